```python
import jax
import jax.numpy as jnp
from jax import lax
import numpy as np

D_MODEL = 2048
BATCH = 8
SEQ = 2048
DEPTH = 4

N_MEM = 256
EPS = 1e-6
DN_QK_HEADS = 16
DN_V_HEADS = 32
DN_HEAD_DIM = 128
DN_QK_WIDTH = DN_QK_HEADS * DN_HEAD_DIM
DN_V_WIDTH = DN_V_HEADS * DN_HEAD_DIM
DN_CONV_WIDTH = 5
DN_CHUNK = 64
DN_CONV_CH = 2 * DN_QK_WIDTH + DN_V_WIDTH
POOL_WINDOWS = (2, 4, 8, 16)
POOL_GROUPS = 4
POOL_WIDTH = DN_V_WIDTH
POOL_GROUP_WIDTH = POOL_WIDTH // POOL_GROUPS
XA_HEADS = 4
XA_HEAD_DIM = D_MODEL // 4
XA_WIDTH = XA_HEADS * XA_HEAD_DIM
INNER = POOL_WIDTH + XA_WIDTH
POOL_IN = POOL_WIDTH + XA_WIDTH + INNER
DN_IN = DN_CONV_CH + XA_WIDTH + INNER + 4 * DN_V_HEADS
N_POOL_LAYERS = (DEPTH + 1) // 2
N_DN_LAYERS = DEPTH // 2

kernel_name = "hybrid_pool_deltanet_memxattn_encoder"


def rmsnorm(x, w):
    xf = x.astype(jnp.float32)
    y = xf * lax.rsqrt(jnp.mean(xf * xf, axis=-1, keepdims=True) + EPS)
    return (y * w.astype(jnp.float32)).astype(x.dtype)


def l2norm(x):
    return x * lax.rsqrt(jnp.sum(x * x, axis=-1, keepdims=True) + EPS)


def memory_cross_attention(xq, mem_k, mem_v):
    b, s, _ = xq.shape
    q = xq.reshape(b, s, XA_HEADS, XA_HEAD_DIM)
    scores = jnp.einsum('bshd,bmhd->bhsm', q, mem_k).astype(jnp.float32) * (XA_HEAD_DIM ** -0.5)
    p = jax.nn.softmax(scores, axis=-1).astype(xq.dtype)
    o = jnp.einsum('bhsm,bmhd->bshd', p, mem_v)
    return o.reshape(b, s, XA_WIDTH)


def multiscale_centred_mean(u):
    s = u.shape[1]
    cs = jnp.cumsum(u.astype(jnp.float32), axis=1)
    cs = jnp.pad(cs, ((0, 0), (1, 0), (0, 0)))
    t = np.arange(s)
    outs = []
    for g, w in enumerate(POOL_WINDOWS):
        lo = np.maximum(t - w // 2, 0)
        hi = np.minimum(t + w // 2, s)
        grp = cs[:, :, g * POOL_GROUP_WIDTH:(g + 1) * POOL_GROUP_WIDTH]
        cnt = jnp.asarray((hi - lo).astype(np.float32))[None, :, None]
        outs.append((grp[:, hi] - grp[:, lo]) / cnt)
    return jnp.concatenate(outs, axis=-1).astype(u.dtype)


def centred_depthwise_conv(u, w):
    c = u.shape[-1]
    k = w.shape[0]
    return lax.conv_general_dilated(
        u, w[:, None, :].astype(u.dtype), window_strides=(1,),
        padding=[(k // 2, k // 2)], dimension_numbers=('NWC', 'WIO', 'NWC'),
        feature_group_count=c)


def chunk_gated_delta(q, k, v, g, beta):
    b, h, s, dk = q.shape
    dv = v.shape[-1]
    c = DN_CHUNK
    n = s // c
    q = q.reshape(b, h, n, c, dk)
    k = k.reshape(b, h, n, c, dk)
    v = v.reshape(b, h, n, c, dv)
    g = jnp.cumsum(g.reshape(b, h, n, c), axis=-1)
    beta = beta.reshape(b, h, n, c)
    k_beta = k * beta[..., None]
    v_beta = v * beta[..., None]
    tril = jnp.tril(jnp.ones((c, c), bool))
    strict = jnp.tril(jnp.ones((c, c), bool), -1)
    diff = g[..., :, None] - g[..., None, :]
    decay = jnp.where(tril, jnp.exp(jnp.where(tril, diff, 0.0)), 0.0)
    lower = jnp.where(strict, jnp.einsum('bhnid,bhnjd->bhnij', k_beta, k) * decay, 0.0)
    a_mat = lower + jnp.eye(c, dtype=lower.dtype)
    rhs = jnp.concatenate([v_beta, k_beta * jnp.exp(g)[..., None]], axis=-1)
    sol = lax.linalg.triangular_solve(a_mat, rhs, left_side=True, lower=True, unit_diagonal=True)
    u_pseudo = sol[..., :dv]
    w_cum = sol[..., dv:]
    attn_intra = jnp.where(tril, jnp.einsum('bhnid,bhnjd->bhnij', q, k) * decay, 0.0)

    def step(state, xs):
        q_c, k_c, u_c, w_c, g_c, a_c = xs
        v_new = u_c - jnp.einsum('bhck,bhkv->bhcv', w_c, state)
        o_c = (jnp.einsum('bhck,bhkv->bhcv', q_c * jnp.exp(g_c)[..., None], state)
               + jnp.einsum('bhcj,bhjv->bhcv', a_c, v_new))
        g_last = g_c[..., -1:]
        k_dec = k_c * jnp.exp(g_last - g_c)[..., None]
        state = state * jnp.exp(g_last)[..., None] + jnp.einsum('bhck,bhcv->bhkv', k_dec, v_new)
        return state, o_c

    xs = tuple(jnp.moveaxis(t, 2, 0) for t in (q, k, u_pseudo, w_cum, g, attn_intra))
    state0 = jnp.zeros((b, h, dk, dv), jnp.float32)
    _, o = lax.scan(step, state0, xs)
    return jnp.moveaxis(o, 0, 2).reshape(b, h, s, dv)


def pooling_branch(h, w_in, w_group, scale, mem_k, mem_v):
    b, s, _ = h.shape
    proj = h @ w_in
    u = proj[..., :POOL_WIDTH]
    xq = proj[..., POOL_WIDTH:POOL_WIDTH + XA_WIDTH]
    gate = proj[..., POOL_WIDTH + XA_WIDTH:]
    pooled = multiscale_centred_mean(u) - u
    mixed = jnp.einsum('bsgc,gcd->bsgd', pooled.reshape(b, s, POOL_GROUPS, POOL_GROUP_WIDTH), w_group)
    mixed = mixed.reshape(b, s, POOL_WIDTH) * scale
    xa = memory_cross_attention(xq, mem_k, mem_v)
    return jnp.concatenate([mixed, xa], axis=-1) * jax.nn.silu(gate)


def deltanet_branch(h, w_in, conv_w, a_log, dt_bias, norm_w, mem_k, mem_v):
    b, s, _ = h.shape
    proj = h @ w_in
    qkv = proj[..., :DN_CONV_CH]
    xq = proj[..., DN_CONV_CH:DN_CONV_CH + XA_WIDTH]
    gate = proj[..., DN_CONV_CH + XA_WIDTH:DN_CONV_CH + XA_WIDTH + INNER]
    ba = proj[..., DN_CONV_CH + XA_WIDTH + INNER:].astype(jnp.float32)
    qkv = jax.nn.silu(centred_depthwise_conv(qkv, conv_w)).astype(jnp.float32)
    q = qkv[..., :DN_QK_WIDTH].reshape(b, s, DN_QK_HEADS, DN_HEAD_DIM)
    k = qkv[..., DN_QK_WIDTH:2 * DN_QK_WIDTH].reshape(b, s, DN_QK_HEADS, DN_HEAD_DIM)
    v = qkv[..., 2 * DN_QK_WIDTH:].reshape(b, s, DN_V_HEADS, DN_HEAD_DIM)
    rep = DN_V_HEADS // DN_QK_HEADS
    q = jnp.repeat(l2norm(q) * (DN_HEAD_DIM ** -0.5), rep, axis=2).transpose(0, 2, 1, 3)
    k = jnp.repeat(l2norm(k), rep, axis=2).transpose(0, 2, 1, 3)
    v = v.transpose(0, 2, 1, 3)
    b_f, b_b, a_f, a_b = jnp.split(ba, 4, axis=-1)

    def decay_gate(a, d):
        g = -jnp.exp(a_log[d].astype(jnp.float32)) * jax.nn.softplus(a + dt_bias[d].astype(jnp.float32))
        return g.transpose(0, 2, 1)

    g_f, g_b = decay_gate(a_f, 0), decay_gate(a_b, 1)
    beta_f = jax.nn.sigmoid(b_f).transpose(0, 2, 1)
    beta_b = jax.nn.sigmoid(b_b).transpose(0, 2, 1)
    o_fwd = chunk_gated_delta(q, k, v, g_f, beta_f)
    flip = lambda t: jnp.flip(t, axis=2)
    o_bwd = flip(chunk_gated_delta(flip(q), flip(k), flip(v), flip(g_b), flip(beta_b)))
    o = o_fwd + o_bwd
    o = o * lax.rsqrt(jnp.mean(o * o, axis=-1, keepdims=True) + EPS) * norm_w.astype(jnp.float32)
    o = o.transpose(0, 2, 1, 3).reshape(b, s, DN_V_WIDTH).astype(h.dtype)
    xa = memory_cross_attention(xq, mem_k, mem_v)
    return jnp.concatenate([o, xa], axis=-1) * jax.nn.silu(gate)


def setup_inputs(seed: int = 0) -> dict:
    key = jax.random.key(seed)
    ks = jax.random.split(key, 16)
    f32 = jnp.float32
    nrm = lambda k, shape, fan_in: jax.random.normal(k, shape, f32) * (fan_in ** -0.5)
    x = jax.random.normal(ks[0], (BATCH, SEQ, D_MODEL), f32)
    mem = jax.random.normal(ks[1], (BATCH, N_MEM, D_MODEL), f32)
    norm_w = 1.0 + 0.05 * jax.random.normal(ks[2], (DEPTH, D_MODEL), f32)
    mem_norm_w = 1.0 + 0.05 * jax.random.normal(ks[3], (DEPTH, D_MODEL), f32)
    w_kv_mem = nrm(ks[4], (DEPTH, D_MODEL, 2 * XA_WIDTH), D_MODEL)
    w_out = nrm(ks[5], (DEPTH, INNER, D_MODEL), INNER)
    pool_w_in = nrm(ks[6], (N_POOL_LAYERS, D_MODEL, POOL_IN), D_MODEL)
    pool_w_group = nrm(ks[7], (N_POOL_LAYERS, POOL_GROUPS, POOL_GROUP_WIDTH, POOL_GROUP_WIDTH), POOL_GROUP_WIDTH)
    pool_scale = 1.0 + 0.1 * jax.random.normal(ks[8], (N_POOL_LAYERS, POOL_WIDTH), f32)
    dn_w_in = nrm(ks[9], (N_DN_LAYERS, D_MODEL, DN_IN), D_MODEL)
    dn_conv_w = nrm(ks[10], (N_DN_LAYERS, DN_CONV_WIDTH, DN_CONV_CH), DN_CONV_WIDTH)
    dn_a_log = jnp.log(jax.random.uniform(ks[11], (N_DN_LAYERS, 2, DN_V_HEADS), f32, minval=1.0, maxval=16.0))
    dt = jnp.exp(jax.random.uniform(ks[12], (N_DN_LAYERS, 2, DN_V_HEADS), f32,
                                    minval=float(np.log(1e-3)), maxval=float(np.log(1e-1))))
    dn_dt_bias = dt + jnp.log(-jnp.expm1(-dt))
    dn_norm_w = 1.0 + 0.05 * jax.random.normal(ks[13], (N_DN_LAYERS, DN_HEAD_DIM), f32)
    final_norm_w = 1.0 + 0.05 * jax.random.normal(ks[14], (D_MODEL,), f32)
    return {'x': x, 'mem': mem, 'norm_w': norm_w, 'mem_norm_w': mem_norm_w, 'w_kv_mem': w_kv_mem,
            'w_out': w_out, 'pool_w_in': pool_w_in, 'pool_w_group': pool_w_group, 'pool_scale': pool_scale,
            'dn_w_in': dn_w_in, 'dn_conv_w': dn_conv_w, 'dn_a_log': dn_a_log, 'dn_dt_bias': dn_dt_bias,
            'dn_norm_w': dn_norm_w, 'final_norm_w': final_norm_w}


def reference(x, mem, norm_w, mem_norm_w, w_kv_mem, w_out, pool_w_in, pool_w_group, pool_scale,
              dn_w_in, dn_conv_w, dn_a_log, dn_dt_bias, dn_norm_w, final_norm_w):
    b, m, _ = mem.shape
    for i in range(DEPTH):
        h = rmsnorm(x, norm_w[i])
        kv = rmsnorm(mem, mem_norm_w[i]) @ w_kv_mem[i]
        mem_k = kv[..., :XA_WIDTH].reshape(b, m, XA_HEADS, XA_HEAD_DIM)
        mem_v = kv[..., XA_WIDTH:].reshape(b, m, XA_HEADS, XA_HEAD_DIM)
        j = i // 2
        if i % 2 == 0:
            y = pooling_branch(h, pool_w_in[j], pool_w_group[j], pool_scale[j], mem_k, mem_v)
        else:
            y = deltanet_branch(h, dn_w_in[j], dn_conv_w[j], dn_a_log[j], dn_dt_bias[j], dn_norm_w[j],
                                mem_k, mem_v)
        x = x + y @ w_out[i]
    return rmsnorm(x, final_norm_w)
```

```python
import functools

import jax
import jax.numpy as jnp
from jax import lax
from jax.experimental import pallas as pl
from jax.experimental.pallas import tpu as pltpu

F32 = jnp.float32
BF16 = jnp.bfloat16

EPS = 1e-6
D_MODEL = 2048
N_MEM = 256
XA_HEADS = 4
XA_HEAD_DIM = 512
XA_WIDTH = XA_HEADS * XA_HEAD_DIM
POOL_GROUPS = 4
POOL_GROUP_WIDTH = 1024
POOL_WIDTH = POOL_GROUPS * POOL_GROUP_WIDTH
DN_QK_HEADS = 16
DN_V_HEADS = 32
DN_HEAD_DIM = 128
DN_QK_WIDTH = DN_QK_HEADS * DN_HEAD_DIM
DN_V_WIDTH = DN_V_HEADS * DN_HEAD_DIM
DN_CONV_CH = 2 * DN_QK_WIDTH + DN_V_WIDTH
DN_CONV_TAPS = 5
DN_CHUNK = 64
DN_PAIR = 2 * DN_CHUNK
INNER = POOL_WIDTH + XA_WIDTH
DN_MAIN = DN_CONV_CH + XA_WIDTH + INNER

VMEM_LIMIT = 56 * 1024 * 1024


def _params(sem):
    return pltpu.CompilerParams(dimension_semantics=sem, vmem_limit_bytes=VMEM_LIMIT)


def _silu(g):
    return g * jax.nn.sigmoid(g)


def _norm_matmul_kernel(x_ref, nw_ref, w_ref, o_ref, h_ref):
    @pl.when(pl.program_id(1) == 0)
    def _():
        xf = x_ref[...]
        ms = jnp.mean(xf * xf, axis=-1, keepdims=True)
        h_ref[...] = (xf * lax.rsqrt(ms + EPS) * nw_ref[...]).astype(h_ref.dtype)

    o_ref[...] = jnp.dot(h_ref[...], w_ref[...], preferred_element_type=F32).astype(o_ref.dtype)


def _norm_matmul(x2d, nw, w, out_dtype, tm, tn):
    m, k = x2d.shape
    n = w.shape[1]
    return pl.pallas_call(
        _norm_matmul_kernel,
        grid=(m // tm, n // tn),
        in_specs=[pl.BlockSpec((tm, k), lambda i, j: (i, 0)),
                  pl.BlockSpec((1, k), lambda i, j: (0, 0)),
                  pl.BlockSpec((k, tn), lambda i, j: (0, j))],
        out_specs=pl.BlockSpec((tm, tn), lambda i, j: (i, j)),
        out_shape=jax.ShapeDtypeStruct((m, n), out_dtype),
        scratch_shapes=[pltpu.VMEM((tm, k), BF16)],
        compiler_params=_params(("parallel", "arbitrary")),
        name="norm_matmul",
    )(x2d, nw.reshape(1, k), w)


def _out_proj_kernel(ym_ref, ya_ref, w1_ref, w2_ref, x_ref, o_ref):
    acc = jnp.dot(ym_ref[...], w1_ref[...], preferred_element_type=F32)
    acc = acc + jnp.dot(ya_ref[...], w2_ref[...], preferred_element_type=F32)
    o_ref[...] = x_ref[...] + acc


def _out_proj(ym, ya, w1, w2, x2d, tm, tn):
    m, n = x2d.shape
    k1, k2 = ym.shape[1], ya.shape[1]
    return pl.pallas_call(
        _out_proj_kernel,
        grid=(n // tn, m // tm),
        in_specs=[pl.BlockSpec((tm, k1), lambda j, i: (i, 0)),
                  pl.BlockSpec((tm, k2), lambda j, i: (i, 0)),
                  pl.BlockSpec((k1, tn), lambda j, i: (0, j)),
                  pl.BlockSpec((k2, tn), lambda j, i: (0, j)),
                  pl.BlockSpec((tm, tn), lambda j, i: (i, j))],
        out_specs=pl.BlockSpec((tm, tn), lambda j, i: (i, j)),
        out_shape=jax.ShapeDtypeStruct((m, n), F32),
        compiler_params=_params(("parallel", "parallel")),
        name="out_proj",
    )(ym, ya, w1, w2, x2d)


def _rmsnorm_kernel(x_ref, nw_ref, o_ref):
    xf = x_ref[...]
    ms = jnp.mean(xf * xf, axis=-1, keepdims=True)
    o_ref[...] = xf * lax.rsqrt(ms + EPS) * nw_ref[...]


def _rmsnorm(x2d, nw, tm):
    m, k = x2d.shape
    return pl.pallas_call(
        _rmsnorm_kernel,
        grid=(m // tm,),
        in_specs=[pl.BlockSpec((tm, k), lambda i: (i, 0)),
                  pl.BlockSpec((1, k), lambda i: (0, 0))],
        out_specs=pl.BlockSpec((tm, k), lambda i: (i, 0)),
        out_shape=jax.ShapeDtypeStruct((m, k), F32),
        compiler_params=_params(("parallel",)),
        name="final_rmsnorm",
    )(x2d, nw.reshape(1, k))


def _xattn_kernel(q_ref, k_ref, v_ref, g_ref, o_ref):
    s = lax.dot_general(q_ref[0], k_ref[0], (((1,), (1,)), ((), ())), preferred_element_type=F32)
    s = s * (XA_HEAD_DIM ** -0.5)
    p = jnp.exp(s - jnp.max(s, axis=-1, keepdims=True))
    l = jnp.sum(p, axis=-1, keepdims=True)
    o = jnp.dot(p.astype(BF16), v_ref[0], preferred_element_type=F32) / l
    o_ref[0] = (o * _silu(g_ref[0].astype(F32))).astype(o_ref.dtype)


def _xattn(proj, kv, q_col, gate_col, ts):
    b, s, _ = proj.shape
    hd = XA_HEAD_DIM
    qb, gb = q_col // hd, gate_col // hd
    return pl.pallas_call(
        _xattn_kernel,
        grid=(b, XA_HEADS, s // ts),
        in_specs=[pl.BlockSpec((1, ts, hd), lambda bi, h, t: (bi, t, qb + h)),
                  pl.BlockSpec((1, N_MEM, hd), lambda bi, h, t: (bi, 0, h)),
                  pl.BlockSpec((1, N_MEM, hd), lambda bi, h, t: (bi, 0, XA_HEADS + h)),
                  pl.BlockSpec((1, ts, hd), lambda bi, h, t: (bi, t, gb + h))],
        out_specs=pl.BlockSpec((1, ts, hd), lambda bi, h, t: (bi, t, h)),
        out_shape=jax.ShapeDtypeStruct((b, s, XA_WIDTH), BF16),
        compiler_params=_params(("parallel", "parallel", "parallel")),
        name="mem_xattn",
    )(proj, kv, kv, proj)


POOL_TILE = 128
POOL_HALO = 64


def _pool_kernel(u_ref, g_ref, wg_ref, sc_ref, o_ref, ext_ref):
    s = u_ref.shape[1]
    gw = u_ref.shape[2]
    grp = pl.program_id(1)
    half = jnp.left_shift(1, grp)
    ext_ref[0:POOL_HALO, :] = jnp.zeros((POOL_HALO, gw), BF16)
    ext_ref[POOL_HALO + s:POOL_HALO + s + POOL_HALO, :] = jnp.zeros((POOL_HALO, gw), BF16)
    ext_ref[POOL_HALO:POOL_HALO + s, :] = u_ref[0]

    kk = POOL_TILE + 2 * POOL_HALO
    r = lax.broadcasted_iota(jnp.int32, (POOL_TILE, kk), 0)
    c = lax.broadcasted_iota(jnp.int32, (POOL_TILE, kk), 1)
    d = c - POOL_HALO - r
    band = jnp.where((d >= -half) & (d < half), 1.0, 0.0).astype(BF16)
    wg = wg_ref[0]
    scale = sc_ref[0]

    def body(i, carry):
        t0 = pl.multiple_of(i * POOL_TILE, POOL_TILE)
        win = ext_ref[pl.ds(t0, kk), :]
        wsum = jnp.dot(band, win, preferred_element_type=F32)
        tok = t0 + lax.broadcasted_iota(jnp.int32, (POOL_TILE, gw), 0)
        cnt = (jnp.minimum(tok + half, s) - jnp.maximum(tok - half, 0)).astype(F32)
        centre = ext_ref[pl.ds(t0 + POOL_HALO, POOL_TILE), :].astype(F32)
        pooled = wsum / cnt - centre
        mixed = jnp.dot(pooled.astype(BF16), wg, preferred_element_type=F32) * scale
        gate = g_ref[0, pl.ds(t0, POOL_TILE), :].astype(F32)
        o_ref[0, pl.ds(t0, POOL_TILE), :] = (mixed * _silu(gate)).astype(o_ref.dtype)
        return carry

    lax.fori_loop(0, s // POOL_TILE, body, 0)


def _pool_mix(proj, wg, scale, gate_col):
    b, s, _ = proj.shape
    gw = POOL_GROUP_WIDTH
    gb = gate_col // gw
    return pl.pallas_call(
        _pool_kernel,
        grid=(b, POOL_GROUPS),
        in_specs=[pl.BlockSpec((1, s, gw), lambda bi, g: (bi, 0, g)),
                  pl.BlockSpec((1, s, gw), lambda bi, g: (bi, 0, gb + g)),
                  pl.BlockSpec((1, gw, gw), lambda bi, g: (g, 0, 0)),
                  pl.BlockSpec((1, 1, gw), lambda bi, g: (g, 0, 0))],
        out_specs=pl.BlockSpec((1, s, gw), lambda bi, g: (bi, 0, g)),
        out_shape=jax.ShapeDtypeStruct((b, s, POOL_WIDTH), BF16),
        scratch_shapes=[pltpu.VMEM((s + 2 * POOL_HALO, gw), BF16)],
        compiler_params=_params(("parallel", "parallel")),
        name="pool_mix",
    )(proj, proj, wg, scale.reshape(POOL_GROUPS, 1, gw))


CONV_PAD = 8


def _dn_prep_kernel(x_ref, cw_ref, *refs, normalize, scale, emit_t):
    if emit_t:
        o_ref, ot_ref, ext_ref = refs
    else:
        o_ref, ext_ref = refs
    s = x_ref.shape[1]
    hd = x_ref.shape[2]
    ext_ref[0:CONV_PAD, :] = jnp.zeros((CONV_PAD, hd), F32)
    ext_ref[CONV_PAD + s:2 * CONV_PAD + s, :] = jnp.zeros((CONV_PAD, hd), F32)
    ext_ref[CONV_PAD:CONV_PAD + s, :] = x_ref[0].astype(F32)
    cw = cw_ref[...]
    tile = 256

    def body(i, carry):
        t0 = pl.multiple_of(i * tile, tile)
        blk = ext_ref[pl.ds(t0, tile + 2 * CONV_PAD), :]
        acc = None
        for tap in range(DN_CONV_TAPS):
            off = CONV_PAD + tap - DN_CONV_TAPS // 2
            term = blk[off:off + tile, :] * cw[tap:tap + 1, :]
            acc = term if acc is None else acc + term
        y = _silu(acc)
        if normalize:
            y = y * lax.rsqrt(jnp.sum(y * y, axis=-1, keepdims=True) + EPS)
            if scale != 1.0:
                y = y * scale
        o_ref[0, pl.ds(t0, tile), :] = y.astype(o_ref.dtype)
        if emit_t:
            for h in range(tile // DN_PAIR):
                ot_ref[0, 0, i * (tile // DN_PAIR) + h] = y[h * DN_PAIR:(h + 1) * DN_PAIR, :].T.astype(ot_ref.dtype)
        return carry

    lax.fori_loop(0, s // tile, body, 0)


def _dn_prep(proj, conv_w, col0, ncols, normalize, scale, emit_t):
    b, s, _ = proj.shape
    hd = DN_HEAD_DIM
    nh = ncols // hd
    cb = col0 // hd
    out_shape = [jax.ShapeDtypeStruct((b, s, ncols), BF16)]
    out_specs = [pl.BlockSpec((1, s, hd), lambda bi, h: (bi, 0, h))]
    if emit_t:
        out_shape.append(jax.ShapeDtypeStruct((b, nh, s // DN_PAIR, hd, DN_PAIR), BF16))
        out_specs.append(pl.BlockSpec((1, 1, s // DN_PAIR, hd, DN_PAIR), lambda bi, h: (bi, h, 0, 0, 0)))
    res = pl.pallas_call(
        functools.partial(_dn_prep_kernel, normalize=normalize, scale=scale, emit_t=emit_t),
        grid=(b, nh),
        in_specs=[pl.BlockSpec((1, s, hd), lambda bi, h: (bi, 0, cb + h)),
                  pl.BlockSpec((DN_CONV_TAPS, hd), lambda bi, h: (0, cb + h))],
        out_specs=out_specs,
        out_shape=out_shape,
        scratch_shapes=[pltpu.VMEM((s + 2 * CONV_PAD, hd), F32)],
        compiler_params=_params(("parallel", "parallel")),
        name="dn_conv",
    )(proj, conv_w)
    return res if emit_t else res[0]


def _dn_gates_kernel(ba_ref, alog_ref, dtb_ref, g_ref, beta_ref, gt_ref):
    s = ba_ref.shape[1]
    nh2 = 2 * DN_V_HEADS
    ba = ba_ref[0]
    beta_ref[0] = jax.nn.sigmoid(ba[:, :nh2])
    z = ba[:, nh2:] + dtb_ref[...]
    softplus = jnp.maximum(z, 0.0) + jnp.log(1.0 + jnp.exp(-jnp.abs(z)))
    g = -jnp.exp(alog_ref[...]) * softplus
    r = lax.broadcasted_iota(jnp.int32, (DN_CHUNK, DN_CHUNK), 0)
    c = lax.broadcasted_iota(jnp.int32, (DN_CHUNK, DN_CHUNK), 1)
    lower = jnp.where(r >= c, 1.0, 0.0).astype(F32)
    upper = jnp.where(r <= c, 1.0, 0.0).astype(F32)
    is_fwd = lax.broadcasted_iota(jnp.int32, (DN_CHUNK, nh2), 1) < DN_V_HEADS
    for n in range(s // DN_CHUNK):
        gc = g[n * DN_CHUNK:(n + 1) * DN_CHUNK, :]
        pre = jnp.dot(lower, gc, preferred_element_type=F32, precision=lax.Precision.HIGHEST)
        suf = jnp.dot(upper, gc, preferred_element_type=F32, precision=lax.Precision.HIGHEST)
        g_ref[0, n * DN_CHUNK:(n + 1) * DN_CHUNK, :] = jnp.where(is_fwd, pre, suf)
        gt_ref[0, n:n + 1, :] = jnp.sum(gc, axis=0, keepdims=True)


def _dn_gates(ba, a_log, dt_bias):
    b, s, w = ba.shape
    nh2 = 2 * DN_V_HEADS
    nchunk = s // DN_CHUNK
    return pl.pallas_call(
        _dn_gates_kernel,
        grid=(b,),
        in_specs=[pl.BlockSpec((1, s, w), lambda bi: (bi, 0, 0)),
                  pl.BlockSpec((1, nh2), lambda bi: (0, 0)),
                  pl.BlockSpec((1, nh2), lambda bi: (0, 0))],
        out_specs=[pl.BlockSpec((1, s, nh2), lambda bi: (bi, 0, 0)),
                   pl.BlockSpec((1, s, nh2), lambda bi: (bi, 0, 0)),
                   pl.BlockSpec((1, nchunk, nh2), lambda bi: (bi, 0, 0))],
        out_shape=[jax.ShapeDtypeStruct((b, s, nh2), F32),
                   jax.ShapeDtypeStruct((b, s, nh2), F32),
                   jax.ShapeDtypeStruct((b, nchunk, nh2), F32)],
        compiler_params=_params(("parallel",)),
        name="dn_gates",
    )(ba, a_log.reshape(1, nh2), dt_bias.reshape(1, nh2))


N_UNITS = 4
ROW_G, ROW_BETA, ROW_GT = 0, 4, 8


def _dn_core_kernel(q_ref, k_ref, kt_ref, v_ref, r_ref, et_ref, gate_ref, nw_ref, o_ref,
                    st_ref, of_ref, ob_ref):
    s = q_ref.shape[1]
    hd = DN_HEAD_DIM
    pr = DN_PAIR
    npair = s // pr
    st_ref[...] = jnp.zeros(st_ref.shape, F32)

    row = lax.broadcasted_iota(jnp.int32, (pr, pr), 0)
    col = lax.broadcasted_iota(jnp.int32, (pr, pr), 1)
    same = jnp.right_shift(row, 6) == jnp.right_shift(col, 6)
    eye = jnp.where(row == col, 1.0, 0.0).astype(F32)
    masks = {False: (same & (row >= col), same & (row > col)),
             True: (same & (row <= col), same & (row < col))}
    zeros_half = jnp.zeros((DN_CHUNK, hd), F32)

    def unit_pair(pi, t0, unit, vh, backward, qk, kk, q2, k2, kt2):
        incl, strict = masks[backward]
        g_row = r_ref[0, 0, pi, ROW_G + unit:ROW_G + unit + 1, :]
        b_row = r_ref[0, 0, pi, ROW_BETA + unit:ROW_BETA + unit + 1, :]
        gt_row = r_ref[0, 0, pi, ROW_GT + unit:ROW_GT + unit + 1, :]
        m2 = jnp.broadcast_to(g_row, (pr, pr))
        m1 = m2.T
        dm = jnp.where(incl, jnp.exp(jnp.where(incl, m1 - m2, 0.0)), 0.0)
        kt_mat = jnp.where(strict, kk * dm, 0.0) * b_row
        a_mat = (qk * dm * b_row).astype(BF16)
        x = eye - kt_mat
        nb = (-kt_mat).astype(BF16)
        pb = jnp.dot(nb, nb, preferred_element_type=F32).astype(BF16)
        for _ in range(4):
            rr = jnp.dot(pb, jnp.concatenate([pb, x.astype(BF16)], axis=1), preferred_element_type=F32)
            x = x + rr[:, pr:]
            pb = rr[:, :pr].astype(BF16)
        x = x + jnp.dot(pb, x.astype(BF16), preferred_element_type=F32)
        e1 = jnp.exp(m1)
        v2 = v_ref[0, pl.ds(t0, pr), vh * hd:(vh + 1) * hd]
        rhs = jnp.concatenate([v2, (k2.astype(F32) * e1).astype(BF16)], axis=1)
        sol = jnp.dot(x.astype(BF16), rhs, preferred_element_type=F32)
        u_t, w_t = sol[:, :hd], sol[:, hd:]
        qd = (q2.astype(F32) * e1).astype(BF16)
        kd = (kt2.astype(F32) * (b_row * jnp.exp(gt_row - g_row))).astype(BF16)
        o_dst = ob_ref if backward else of_ref
        for c in ((1, 0) if backward else (0, 1)):
            rows = slice(c * DN_CHUNK, (c + 1) * DN_CHUNK)
            st = st_ref[unit]
            wq = jnp.concatenate([w_t[rows].astype(BF16), qd[rows]], axis=0)
            rs = jnp.dot(wq, st.astype(BF16), preferred_element_type=F32)
            v_new = u_t[rows] - rs[:DN_CHUNK]
            halves = [zeros_half, zeros_half]
            halves[c] = v_new
            v_full = jnp.concatenate(halves, axis=0).astype(BF16)
            o_c = rs[DN_CHUNK:] + jnp.dot(a_mat[rows], v_full, preferred_element_type=F32)
            o_dst[vh, pl.ds(t0 + c * DN_CHUNK, DN_CHUNK), :] = o_c
            decay_tot = et_ref[0, 0, pl.ds(unit * (2 * npair) + 2 * pi + c, 1), :]
            st_ref[unit] = st * decay_tot + jnp.dot(kd, v_full, preferred_element_type=F32)

    def body(it, carry):
        for backward in (False, True):
            pi = (npair - 1 - it) if backward else it
            t0 = pl.multiple_of(pi * pr, pr)
            q2 = q_ref[0, pl.ds(t0, pr), :]
            k2 = k_ref[0, pl.ds(t0, pr), :]
            kt2 = kt_ref[0, 0, pi]
            both = jnp.dot(jnp.concatenate([q2, k2], axis=0), kt2, preferred_element_type=F32)
            qk, kk = both[:pr], both[pr:]
            for vh in range(2):
                unit_pair(pi, t0, 2 * int(backward) + vh, vh, backward, qk, kk, q2, k2, kt2)
        return carry

    lax.fori_loop(0, npair, body, 0)

    tile = 256
    nw = nw_ref[...]

    def epilogue(i, carry):
        t0 = pl.multiple_of(i * tile, tile)
        for vh in range(2):
            o = of_ref[vh, pl.ds(t0, tile), :] + ob_ref[vh, pl.ds(t0, tile), :]
            o = o * lax.rsqrt(jnp.mean(o * o, axis=-1, keepdims=True) + EPS) * nw
            gate = gate_ref[0, pl.ds(t0, tile), vh * hd:(vh + 1) * hd].astype(F32)
            o_ref[0, pl.ds(t0, tile), vh * hd:(vh + 1) * hd] = (o * _silu(gate)).astype(o_ref.dtype)
        return carry

    lax.fori_loop(0, s // tile, epilogue, 0)


def _dn_core(qn, kn, knt, vs, rows, etot, proj, norm_w, gate_col):
    b, s, _ = qn.shape
    hd = DN_HEAD_DIM
    npair = s // DN_PAIR
    gb = gate_col // (2 * hd)
    return pl.pallas_call(
        _dn_core_kernel,
        grid=(b, DN_QK_HEADS),
        in_specs=[pl.BlockSpec((1, s, hd), lambda bi, j: (bi, 0, j)),
                  pl.BlockSpec((1, s, hd), lambda bi, j: (bi, 0, j)),
                  pl.BlockSpec((1, 1, npair, hd, DN_PAIR), lambda bi, j: (bi, j, 0, 0, 0)),
                  pl.BlockSpec((1, s, 2 * hd), lambda bi, j: (bi, 0, j)),
                  pl.BlockSpec((1, 1, npair, 16, DN_PAIR), lambda bi, j: (bi, j, 0, 0, 0)),
                  pl.BlockSpec((1, 1, N_UNITS * 2 * npair, hd), lambda bi, j: (bi, j, 0, 0)),
                  pl.BlockSpec((1, s, 2 * hd), lambda bi, j: (bi, 0, gb + j)),
                  pl.BlockSpec((1, hd), lambda bi, j: (0, 0))],
        out_specs=pl.BlockSpec((1, s, 2 * hd), lambda bi, j: (bi, 0, j)),
        out_shape=jax.ShapeDtypeStruct((b, s, DN_V_WIDTH), BF16),
        scratch_shapes=[pltpu.VMEM((N_UNITS, hd, hd), F32),
                        pltpu.VMEM((2, s, hd), F32),
                        pltpu.VMEM((2, s, hd), F32)],
        compiler_params=_params(("parallel", "parallel")),
        name="dn_core",
    )(qn, kn, knt, vs, rows, etot, proj, norm_w.reshape(1, hd))


def _per_unit(a):
    b, t, _ = a.shape
    a = a.reshape(b, t, 2, DN_QK_HEADS, 2)
    return a.transpose(0, 3, 2, 4, 1).reshape(b, DN_QK_HEADS, N_UNITS, t)


def _deltanet_mixer(proj, ba, conv_w, a_log, dt_bias, norm_w, gate_col):
    b, s, _ = proj.shape
    npair = s // DN_PAIR
    nchunk = s // DN_CHUNK
    qn = _dn_prep(proj, conv_w, 0, DN_QK_WIDTH, True, DN_HEAD_DIM ** -0.5, False)
    kn, knt = _dn_prep(proj, conv_w, DN_QK_WIDTH, DN_QK_WIDTH, True, 1.0, True)
    vs = _dn_prep(proj, conv_w, 2 * DN_QK_WIDTH, DN_V_WIDTH, False, 1.0, False)
    g_cum, beta, g_tot = _dn_gates(ba, a_log, dt_bias)
    gt_tok = jnp.broadcast_to(g_tot[:, :, None, :], (b, nchunk, DN_CHUNK, g_tot.shape[-1])).reshape(b, s, -1)
    rows = jnp.concatenate([_per_unit(g_cum), _per_unit(beta), _per_unit(gt_tok),
                            jnp.zeros((b, DN_QK_HEADS, N_UNITS, s), F32)], axis=2)
    rows = rows.reshape(b, DN_QK_HEADS, 16, npair, DN_PAIR).transpose(0, 1, 3, 2, 4)
    etot = jnp.exp(_per_unit(g_tot)).reshape(b, DN_QK_HEADS, N_UNITS * nchunk)
    etot = jnp.broadcast_to(etot[..., None], (b, DN_QK_HEADS, N_UNITS * nchunk, DN_HEAD_DIM))
    return _dn_core(qn, kn, knt, vs, rows, etot, proj, norm_w, gate_col)


def kernel(x, mem, norm_w, mem_norm_w, w_kv_mem, w_out, pool_w_in, pool_w_group, pool_scale,
           dn_w_in, dn_conv_w, dn_a_log, dn_dt_bias, dn_norm_w, final_norm_w):
    b, s, d = x.shape
    m = mem.shape[1]
    depth = norm_w.shape[0]
    x2d = x.reshape(b * s, d)
    mem2d = mem.reshape(b * m, d)
    tm = min(1024, b * s)
    for i in range(depth):
        j = i // 2
        kv = _norm_matmul(mem2d, mem_norm_w[i], w_kv_mem[i].astype(BF16), BF16, min(1024, b * m), 1024)
        kv = kv.reshape(b, m, 2 * XA_WIDTH)
        w1 = w_out[i][:POOL_WIDTH].astype(BF16)
        w2 = w_out[i][POOL_WIDTH:].astype(BF16)
        if i % 2 == 0:
            proj = _norm_matmul(x2d, norm_w[i], pool_w_in[j].astype(BF16), BF16, tm, 1024)
            proj = proj.reshape(b, s, -1)
            gate_col = POOL_WIDTH + XA_WIDTH
            ym = _pool_mix(proj, pool_w_group[j].astype(BF16), pool_scale[j], gate_col)
            ya = _xattn(proj, kv, POOL_WIDTH, gate_col + POOL_WIDTH, min(512, s))
        else:
            w_in = dn_w_in[j]
            proj = _norm_matmul(x2d, norm_w[i], w_in[:, :DN_MAIN].astype(BF16), BF16, tm, 1024)
            proj = proj.reshape(b, s, -1)
            ba = _norm_matmul(x2d, norm_w[i], w_in[:, DN_MAIN:].astype(BF16), F32, tm, 4 * DN_V_HEADS)
            ba = ba.reshape(b, s, -1)
            gate_col = DN_CONV_CH + XA_WIDTH
            ym = _deltanet_mixer(proj, ba, dn_conv_w[j], dn_a_log[j], dn_dt_bias[j], dn_norm_w[j], gate_col)
            ya = _xattn(proj, kv, DN_CONV_CH, gate_col + DN_V_WIDTH, min(512, s))
        x2d = _out_proj(ym.reshape(b * s, -1), ya.reshape(b * s, -1), w1, w2, x2d, min(512, b * s), 1024)
    return _rmsnorm(x2d, final_norm_w, min(512, b * s)).reshape(b, s, d)
```

```python
import functools

import jax
import jax.numpy as jnp
from jax import lax
from jax.experimental import pallas as pl
from jax.experimental.pallas import tpu as pltpu

F32 = jnp.float32
BF16 = jnp.bfloat16

EPS = 1e-6
D_MODEL = 2048
N_MEM = 256
XA_HEADS = 4
XA_HEAD_DIM = 512
XA_WIDTH = XA_HEADS * XA_HEAD_DIM
POOL_GROUPS = 4
POOL_GROUP_WIDTH = 1024
POOL_WIDTH = POOL_GROUPS * POOL_GROUP_WIDTH
DN_QK_HEADS = 16
DN_V_HEADS = 32
DN_HEAD_DIM = 128
DN_QK_WIDTH = DN_QK_HEADS * DN_HEAD_DIM
DN_V_WIDTH = DN_V_HEADS * DN_HEAD_DIM
DN_CONV_CH = 2 * DN_QK_WIDTH + DN_V_WIDTH
DN_CONV_TAPS = 5
DN_CHUNK = 64
DN_PAIR = 2 * DN_CHUNK
INNER = POOL_WIDTH + XA_WIDTH
DN_MAIN = DN_CONV_CH + XA_WIDTH + INNER

VMEM_LIMIT = 56 * 1024 * 1024


def _params(sem):
    return pltpu.CompilerParams(dimension_semantics=sem, vmem_limit_bytes=VMEM_LIMIT)


def _silu(g):
    return g * jax.nn.sigmoid(g)


def _norm_matmul_kernel(x_ref, nw_ref, w_ref, o_ref, h_ref):
    @pl.when(pl.program_id(1) == 0)
    def _():
        xf = x_ref[...]
        ms = jnp.mean(xf * xf, axis=-1, keepdims=True)
        h_ref[...] = (xf * lax.rsqrt(ms + EPS) * nw_ref[...]).astype(h_ref.dtype)

    o_ref[...] = jnp.dot(h_ref[...], w_ref[...], preferred_element_type=F32).astype(o_ref.dtype)


def _norm_matmul(x2d, nw, w, out_dtype, tm, tn):
    m, k = x2d.shape
    n = w.shape[1]
    return pl.pallas_call(
        _norm_matmul_kernel,
        grid=(m // tm, n // tn),
        in_specs=[pl.BlockSpec((tm, k), lambda i, j: (i, 0)),
                  pl.BlockSpec((1, k), lambda i, j: (0, 0)),
                  pl.BlockSpec((k, tn), lambda i, j: (0, j))],
        out_specs=pl.BlockSpec((tm, tn), lambda i, j: (i, j)),
        out_shape=jax.ShapeDtypeStruct((m, n), out_dtype),
        scratch_shapes=[pltpu.VMEM((tm, k), BF16)],
        compiler_params=_params(("parallel", "arbitrary")),
        name="norm_matmul",
    )(x2d, nw.reshape(1, k), w)


def _out_proj_kernel(ym_ref, ya_ref, w1_ref, w2_ref, x_ref, o_ref):
    acc = jnp.dot(ym_ref[...], w1_ref[...], preferred_element_type=F32)
    acc = acc + jnp.dot(ya_ref[...], w2_ref[...], preferred_element_type=F32)
    o_ref[...] = x_ref[...] + acc


def _out_proj(ym, ya, w1, w2, x2d, tm, tn):
    m, n = x2d.shape
    k1, k2 = ym.shape[1], ya.shape[1]
    return pl.pallas_call(
        _out_proj_kernel,
        grid=(n // tn, m // tm),
        in_specs=[pl.BlockSpec((tm, k1), lambda j, i: (i, 0)),
                  pl.BlockSpec((tm, k2), lambda j, i: (i, 0)),
                  pl.BlockSpec((k1, tn), lambda j, i: (0, j)),
                  pl.BlockSpec((k2, tn), lambda j, i: (0, j)),
                  pl.BlockSpec((tm, tn), lambda j, i: (i, j))],
        out_specs=pl.BlockSpec((tm, tn), lambda j, i: (i, j)),
        out_shape=jax.ShapeDtypeStruct((m, n), F32),
        compiler_params=_params(("parallel", "parallel")),
        name="out_proj",
    )(ym, ya, w1, w2, x2d)


def _rmsnorm_kernel(x_ref, nw_ref, o_ref):
    xf = x_ref[...]
    ms = jnp.mean(xf * xf, axis=-1, keepdims=True)
    o_ref[...] = xf * lax.rsqrt(ms + EPS) * nw_ref[...]


def _rmsnorm(x2d, nw, tm):
    m, k = x2d.shape
    return pl.pallas_call(
        _rmsnorm_kernel,
        grid=(m // tm,),
        in_specs=[pl.BlockSpec((tm, k), lambda i: (i, 0)),
                  pl.BlockSpec((1, k), lambda i: (0, 0))],
        out_specs=pl.BlockSpec((tm, k), lambda i: (i, 0)),
        out_shape=jax.ShapeDtypeStruct((m, k), F32),
        compiler_params=_params(("parallel",)),
        name="final_rmsnorm",
    )(x2d, nw.reshape(1, k))


def _xattn_kernel(q_ref, k_ref, v_ref, g_ref, o_ref):
    s = lax.dot_general(q_ref[0], k_ref[0], (((1,), (1,)), ((), ())), preferred_element_type=F32)
    s = s * (XA_HEAD_DIM ** -0.5)
    p = jnp.exp(s - jnp.max(s, axis=-1, keepdims=True))
    l = jnp.sum(p, axis=-1, keepdims=True)
    o = jnp.dot(p.astype(BF16), v_ref[0], preferred_element_type=F32) / l
    o_ref[0] = (o * _silu(g_ref[0].astype(F32))).astype(o_ref.dtype)


def _xattn(proj, kv, q_col, gate_col, ts):
    b, s, _ = proj.shape
    hd = XA_HEAD_DIM
    qb, gb = q_col // hd, gate_col // hd
    return pl.pallas_call(
        _xattn_kernel,
        grid=(b, XA_HEADS, s // ts),
        in_specs=[pl.BlockSpec((1, ts, hd), lambda bi, h, t: (bi, t, qb + h)),
                  pl.BlockSpec((1, N_MEM, hd), lambda bi, h, t: (bi, 0, h)),
                  pl.BlockSpec((1, N_MEM, hd), lambda bi, h, t: (bi, 0, XA_HEADS + h)),
                  pl.BlockSpec((1, ts, hd), lambda bi, h, t: (bi, t, gb + h))],
        out_specs=pl.BlockSpec((1, ts, hd), lambda bi, h, t: (bi, t, h)),
        out_shape=jax.ShapeDtypeStruct((b, s, XA_WIDTH), BF16),
        compiler_params=_params(("parallel", "parallel", "parallel")),
        name="mem_xattn",
    )(proj, kv, kv, proj)


POOL_TILE = 128
POOL_HALO = 64


def _pool_kernel(u_ref, g_ref, wg_ref, sc_ref, o_ref, ext_ref):
    s = u_ref.shape[1]
    gw = u_ref.shape[2]
    grp = pl.program_id(1)
    half = jnp.left_shift(1, grp)
    ext_ref[0:POOL_HALO, :] = jnp.zeros((POOL_HALO, gw), BF16)
    ext_ref[POOL_HALO + s:POOL_HALO + s + POOL_HALO, :] = jnp.zeros((POOL_HALO, gw), BF16)
    ext_ref[POOL_HALO:POOL_HALO + s, :] = u_ref[0]

    kk = POOL_TILE + 2 * POOL_HALO
    r = lax.broadcasted_iota(jnp.int32, (POOL_TILE, kk), 0)
    c = lax.broadcasted_iota(jnp.int32, (POOL_TILE, kk), 1)
    d = c - POOL_HALO - r
    band = jnp.where((d >= -half) & (d < half), 1.0, 0.0).astype(BF16)
    wg = wg_ref[0]
    scale = sc_ref[0]

    def body(i, carry):
        t0 = pl.multiple_of(i * POOL_TILE, POOL_TILE)
        win = ext_ref[pl.ds(t0, kk), :]
        wsum = jnp.dot(band, win, preferred_element_type=F32)
        tok = t0 + lax.broadcasted_iota(jnp.int32, (POOL_TILE, gw), 0)
        cnt = (jnp.minimum(tok + half, s) - jnp.maximum(tok - half, 0)).astype(F32)
        centre = ext_ref[pl.ds(t0 + POOL_HALO, POOL_TILE), :].astype(F32)
        pooled = wsum / cnt - centre
        mixed = jnp.dot(pooled.astype(BF16), wg, preferred_element_type=F32) * scale
        gate = g_ref[0, pl.ds(t0, POOL_TILE), :].astype(F32)
        o_ref[0, pl.ds(t0, POOL_TILE), :] = (mixed * _silu(gate)).astype(o_ref.dtype)
        return carry

    lax.fori_loop(0, s // POOL_TILE, body, 0)


def _pool_mix(proj, wg, scale, gate_col):
    b, s, _ = proj.shape
    gw = POOL_GROUP_WIDTH
    gb = gate_col // gw
    return pl.pallas_call(
        _pool_kernel,
        grid=(b, POOL_GROUPS),
        in_specs=[pl.BlockSpec((1, s, gw), lambda bi, g: (bi, 0, g)),
                  pl.BlockSpec((1, s, gw), lambda bi, g: (bi, 0, gb + g)),
                  pl.BlockSpec((1, gw, gw), lambda bi, g: (g, 0, 0)),
                  pl.BlockSpec((1, 1, gw), lambda bi, g: (g, 0, 0))],
        out_specs=pl.BlockSpec((1, s, gw), lambda bi, g: (bi, 0, g)),
        out_shape=jax.ShapeDtypeStruct((b, s, POOL_WIDTH), BF16),
        scratch_shapes=[pltpu.VMEM((s + 2 * POOL_HALO, gw), BF16)],
        compiler_params=_params(("parallel", "parallel")),
        name="pool_mix",
    )(proj, proj, wg, scale.reshape(POOL_GROUPS, 1, gw))


CONV_PAD = 8


def _dn_prep_kernel(x_ref, cw_ref, *refs, normalize, scale, emit_t):
    if emit_t:
        o_ref, ot_ref, ext_ref = refs
    else:
        o_ref, ext_ref = refs
    s = x_ref.shape[1]
    hd = x_ref.shape[2]
    ext_ref[0:CONV_PAD, :] = jnp.zeros((CONV_PAD, hd), F32)
    ext_ref[CONV_PAD + s:2 * CONV_PAD + s, :] = jnp.zeros((CONV_PAD, hd), F32)
    ext_ref[CONV_PAD:CONV_PAD + s, :] = x_ref[0].astype(F32)
    cw = cw_ref[...]
    tile = 256

    def body(i, carry):
        t0 = pl.multiple_of(i * tile, tile)
        blk = ext_ref[pl.ds(t0, tile + 2 * CONV_PAD), :]
        acc = None
        for tap in range(DN_CONV_TAPS):
            off = CONV_PAD + tap - DN_CONV_TAPS // 2
            term = blk[off:off + tile, :] * cw[tap:tap + 1, :]
            acc = term if acc is None else acc + term
        y = _silu(acc)
        if normalize:
            y = y * lax.rsqrt(jnp.sum(y * y, axis=-1, keepdims=True) + EPS)
            if scale != 1.0:
                y = y * scale
        o_ref[0, pl.ds(t0, tile), :] = y.astype(o_ref.dtype)
        if emit_t:
            for h in range(tile // DN_PAIR):
                ot_ref[0, 0, i * (tile // DN_PAIR) + h] = y[h * DN_PAIR:(h + 1) * DN_PAIR, :].T.astype(ot_ref.dtype)
        return carry

    lax.fori_loop(0, s // tile, body, 0)


def _dn_prep(proj, conv_w, col0, ncols, normalize, scale, emit_t):
    b, s, _ = proj.shape
    hd = DN_HEAD_DIM
    nh = ncols // hd
    cb = col0 // hd
    out_shape = [jax.ShapeDtypeStruct((b, s, ncols), BF16)]
    out_specs = [pl.BlockSpec((1, s, hd), lambda bi, h: (bi, 0, h))]
    if emit_t:
        out_shape.append(jax.ShapeDtypeStruct((b, nh, s // DN_PAIR, hd, DN_PAIR), BF16))
        out_specs.append(pl.BlockSpec((1, 1, s // DN_PAIR, hd, DN_PAIR), lambda bi, h: (bi, h, 0, 0, 0)))
    res = pl.pallas_call(
        functools.partial(_dn_prep_kernel, normalize=normalize, scale=scale, emit_t=emit_t),
        grid=(b, nh),
        in_specs=[pl.BlockSpec((1, s, hd), lambda bi, h: (bi, 0, cb + h)),
                  pl.BlockSpec((DN_CONV_TAPS, hd), lambda bi, h: (0, cb + h))],
        out_specs=out_specs,
        out_shape=out_shape,
        scratch_shapes=[pltpu.VMEM((s + 2 * CONV_PAD, hd), F32)],
        compiler_params=_params(("parallel", "parallel")),
        name="dn_conv",
    )(proj, conv_w)
    return res if emit_t else res[0]


def _dn_gates_kernel(ba_ref, alog_ref, dtb_ref, g_ref, beta_ref, gt_ref, et_ref):
    s = ba_ref.shape[1]
    nh2 = 2 * DN_V_HEADS
    ba = ba_ref[0]
    beta_ref[0] = jax.nn.sigmoid(ba[:, :nh2])
    z = ba[:, nh2:] + dtb_ref[...]
    softplus = jnp.maximum(z, 0.0) + jnp.log(1.0 + jnp.exp(-jnp.abs(z)))
    g = -jnp.exp(alog_ref[...]) * softplus
    r = lax.broadcasted_iota(jnp.int32, (DN_CHUNK, DN_CHUNK), 0)
    c = lax.broadcasted_iota(jnp.int32, (DN_CHUNK, DN_CHUNK), 1)
    lower = jnp.where(r >= c, 1.0, 0.0).astype(F32)
    upper = jnp.where(r <= c, 1.0, 0.0).astype(F32)
    is_fwd = lax.broadcasted_iota(jnp.int32, (DN_CHUNK, nh2), 1) < DN_V_HEADS
    for n in range(s // DN_CHUNK):
        gc = g[n * DN_CHUNK:(n + 1) * DN_CHUNK, :]
        pre = jnp.dot(lower, gc, preferred_element_type=F32, precision=lax.Precision.HIGHEST)
        suf = jnp.dot(upper, gc, preferred_element_type=F32, precision=lax.Precision.HIGHEST)
        g_ref[0, n * DN_CHUNK:(n + 1) * DN_CHUNK, :] = jnp.where(is_fwd, pre, suf)
        g_tot = jnp.sum(gc, axis=0, keepdims=True)
        gt_ref[0, n:n + 1, :] = g_tot
        et_ref[0, n:n + 1, :] = jnp.exp(g_tot)


def _dn_gates(ba, a_log, dt_bias):
    b, s, w = ba.shape
    nh2 = 2 * DN_V_HEADS
    nchunk = s // DN_CHUNK
    return pl.pallas_call(
        _dn_gates_kernel,
        grid=(b,),
        in_specs=[pl.BlockSpec((1, s, w), lambda bi: (bi, 0, 0)),
                  pl.BlockSpec((1, nh2), lambda bi: (0, 0)),
                  pl.BlockSpec((1, nh2), lambda bi: (0, 0))],
        out_specs=[pl.BlockSpec((1, s, nh2), lambda bi: (bi, 0, 0)),
                   pl.BlockSpec((1, s, nh2), lambda bi: (bi, 0, 0)),
                   pl.BlockSpec((1, nchunk, nh2), lambda bi: (bi, 0, 0)),
                   pl.BlockSpec((1, nchunk, nh2), lambda bi: (bi, 0, 0))],
        out_shape=[jax.ShapeDtypeStruct((b, s, nh2), F32),
                   jax.ShapeDtypeStruct((b, s, nh2), F32),
                   jax.ShapeDtypeStruct((b, nchunk, nh2), F32),
                   jax.ShapeDtypeStruct((b, nchunk, nh2), F32)],
        compiler_params=_params(("parallel",)),
        name="dn_gates",
    )(ba, a_log.reshape(1, nh2), dt_bias.reshape(1, nh2))


N_UNITS = 4
DN_HEADS_PER_STEP = 2
DN_PAIRS_PER_ITER = 2
ROW_G, ROW_BETA, ROW_GT = 0, 4, 8


def _dn_core_kernel(q_ref, k_ref, kt_ref, v_ref, r_ref, et_ref, gate_ref, nw_ref, o_ref,
                    u_sc, w_sc, a_sc, qd_sc, kd_sc, st_ref, of_ref, ob_ref):
    s = q_ref.shape[1]
    hd = DN_HEAD_DIM
    pr = DN_PAIR
    npair = s // pr
    nchunk = 2 * npair

    row = lax.broadcasted_iota(jnp.int32, (pr, pr), 0)
    col = lax.broadcasted_iota(jnp.int32, (pr, pr), 1)
    same = jnp.right_shift(row, 6) == jnp.right_shift(col, 6)
    eye = jnp.where(row == col, 1.0, 0.0).astype(F32)
    masks = {False: (same & (row >= col), same & (row > col)),
             True: (same & (row <= col), same & (row < col))}
    zeros_half = jnp.zeros((DN_CHUNK, hd), F32)

    def mm(a, b):
        return jnp.dot(a, b, preferred_element_type=F32)


    nheads = DN_HEADS_PER_STEP

    def phase1(it, carry):
        heads = []
        for sp in range(DN_PAIRS_PER_ITER):
            pi = it * DN_PAIRS_PER_ITER + sp
            t0 = pl.multiple_of(pi * pr, pr)
            for hl in range(nheads):
                heads.append((hl, pi, q_ref[0, pl.ds(t0, pr), hl * hd:(hl + 1) * hd],
                              k_ref[0, pl.ds(t0, pr), hl * hd:(hl + 1) * hd], kt_ref[0, hl, pi],
                              v_ref[0, pl.ds(t0, pr), 2 * hl * hd:2 * (hl + 1) * hd], r_ref[0, hl, pi]))
        both = [mm(jnp.concatenate([q2, k2], axis=0), kt2) for _, _, q2, k2, kt2, _, _ in heads]
        chains = []
        for (hl, pi, q2, k2, kt2, v2, rows), qkk in zip(heads, both):
            qk, kk = qkk[:pr], qkk[pr:]
            q2f, k2f, kt2f = q2.astype(F32), k2.astype(F32), kt2.astype(F32)
            for sub in range(N_UNITS):
                unit = hl * N_UNITS + sub
                incl, strict = masks[sub >= 2]
                g_row = rows[ROW_G + sub:ROW_G + sub + 1]
                b_row = rows[ROW_BETA + sub:ROW_BETA + sub + 1]
                gt_row = rows[ROW_GT + sub:ROW_GT + sub + 1]
                m2 = jnp.broadcast_to(g_row, (pr, pr))
                m1 = m2.T
                dm = jnp.where(incl, jnp.exp(jnp.where(incl, m1 - m2, 0.0)), 0.0)
                kt_mat = jnp.where(strict, kk * dm, 0.0) * b_row
                e1 = jnp.exp(m1)
                vh = sub % 2
                a_sc[unit, pi] = (qk * dm * b_row).astype(BF16)
                qd_sc[unit, pi] = (q2f * e1).astype(BF16)
                kd_sc[unit, pi] = (kt2f * (b_row * jnp.exp(gt_row - g_row))).astype(BF16)
                chains.append(dict(
                    unit=unit, pi=pi, x=eye - kt_mat, nb=(-kt_mat).astype(BF16),
                    rhs=jnp.concatenate([v2[:, vh * hd:(vh + 1) * hd], (k2f * e1).astype(BF16)], axis=1)))
        for ch, n2 in zip(chains, [mm(ch["nb"], ch["nb"]) for ch in chains]):
            ch["pb"] = n2.astype(BF16)
        for _ in range(4):
            rrs = [mm(ch["pb"], jnp.concatenate([ch["pb"], ch["x"].astype(BF16)], axis=1)) for ch in chains]
            for ch, rr in zip(chains, rrs):
                ch["x"] = ch["x"] + rr[:, pr:]
                ch["pb"] = rr[:, :pr].astype(BF16)
        for ch, last in zip(chains, [mm(ch["pb"], ch["x"].astype(BF16)) for ch in chains]):
            ch["x"] = ch["x"] + last
        sols = [mm(ch["x"].astype(BF16), ch["rhs"]) for ch in chains]
        for ch, sol in zip(chains, sols):
            unit, pi = ch["unit"], ch["pi"]
            u_sc[unit, pi] = sol[:, :hd]
            w_sc[unit, pi] = sol[:, hd:].astype(BF16)
        return carry

    lax.fori_loop(0, npair // DN_PAIRS_PER_ITER, phase1, 0)

    st_ref[...] = jnp.zeros(st_ref.shape, F32)

    def phase2(it, carry):
        units = []
        for unit in range(nheads * N_UNITS):
            hl, sub = divmod(unit, N_UNITS)
            backward = sub >= 2
            pi = (npair - 1 - it) if backward else it
            units.append(dict(
                unit=unit, backward=backward, pi=pi, vhead=2 * hl + sub % 2,
                u=u_sc[unit, pi], w=w_sc[unit, pi], a=a_sc[unit, pi],
                qd=qd_sc[unit, pi], kd=kd_sc[unit, pi], st=st_ref[unit], outs=[None, None],
                decay=[et_ref[0, hl, pl.ds(sub * nchunk + 2 * pi + c, 1), :] for c in range(2)]))
        for step in range(2):
            for un in units:
                c = (1 - step) if un["backward"] else step
                un["c"] = c
                un["rows"] = slice(c * DN_CHUNK, (c + 1) * DN_CHUNK)
            rss = [mm(jnp.concatenate([un["w"][un["rows"]], un["qd"][un["rows"]]], axis=0),
                      un["st"].astype(BF16)) for un in units]
            for un, rs in zip(units, rss):
                halves = [zeros_half, zeros_half]
                halves[un["c"]] = un["u"][un["rows"]] - rs[:DN_CHUNK]
                un["v_full"] = jnp.concatenate(halves, axis=0).astype(BF16)
                un["qs"] = rs[DN_CHUNK:]
            intra = [mm(un["a"][un["rows"]], un["v_full"]) for un in units]
            upd = [mm(un["kd"], un["v_full"]) for un in units]
            for un, o_in, st_up in zip(units, intra, upd):
                un["outs"][un["c"]] = un["qs"] + o_in
                un["st"] = un["st"] * un["decay"][un["c"]] + st_up
        for un in units:
            t0 = pl.multiple_of(un["pi"] * pr, pr)
            dst = ob_ref if un["backward"] else of_ref
            dst[un["vhead"], pl.ds(t0, pr), :] = jnp.concatenate(un["outs"], axis=0)
            st_ref[un["unit"]] = un["st"]
        return carry

    lax.fori_loop(0, npair, phase2, 0)

    tile = 256
    nw = nw_ref[...]

    def epilogue(i, carry):
        t0 = pl.multiple_of(i * tile, tile)
        for vh in range(2 * nheads):
            o = of_ref[vh, pl.ds(t0, tile), :] + ob_ref[vh, pl.ds(t0, tile), :]
            o = o * lax.rsqrt(jnp.mean(o * o, axis=-1, keepdims=True) + EPS) * nw
            gate = gate_ref[0, pl.ds(t0, tile), vh * hd:(vh + 1) * hd].astype(F32)
            o_ref[0, pl.ds(t0, tile), vh * hd:(vh + 1) * hd] = (o * _silu(gate)).astype(o_ref.dtype)
        return carry

    lax.fori_loop(0, s // tile, epilogue, 0)


def _dn_core(qn, kn, knt, vs, rows, etot, proj, norm_w, gate_col):
    b, s, _ = qn.shape
    hd = DN_HEAD_DIM
    npair = s // DN_PAIR
    nh = DN_HEADS_PER_STEP
    nu = nh * N_UNITS
    gb = gate_col // (2 * nh * hd)
    return pl.pallas_call(
        _dn_core_kernel,
        grid=(b, DN_QK_HEADS // nh),
        in_specs=[pl.BlockSpec((1, s, nh * hd), lambda bi, j: (bi, 0, j)),
                  pl.BlockSpec((1, s, nh * hd), lambda bi, j: (bi, 0, j)),
                  pl.BlockSpec((1, nh, npair, hd, DN_PAIR), lambda bi, j: (bi, j, 0, 0, 0)),
                  pl.BlockSpec((1, s, 2 * nh * hd), lambda bi, j: (bi, 0, j)),
                  pl.BlockSpec((1, nh, npair, 16, DN_PAIR), lambda bi, j: (bi, j, 0, 0, 0)),
                  pl.BlockSpec((1, nh, N_UNITS * 2 * npair, hd), lambda bi, j: (bi, j, 0, 0)),
                  pl.BlockSpec((1, s, 2 * nh * hd), lambda bi, j: (bi, 0, gb + j)),
                  pl.BlockSpec((1, hd), lambda bi, j: (0, 0))],
        out_specs=pl.BlockSpec((1, s, 2 * nh * hd), lambda bi, j: (bi, 0, j)),
        out_shape=jax.ShapeDtypeStruct((b, s, DN_V_WIDTH), BF16),
        scratch_shapes=[pltpu.VMEM((nu, npair, DN_PAIR, hd), F32),
                        pltpu.VMEM((nu, npair, DN_PAIR, hd), BF16),
                        pltpu.VMEM((nu, npair, DN_PAIR, DN_PAIR), BF16),
                        pltpu.VMEM((nu, npair, DN_PAIR, hd), BF16),
                        pltpu.VMEM((nu, npair, hd, DN_PAIR), BF16),
                        pltpu.VMEM((nu, hd, hd), F32),
                        pltpu.VMEM((2 * nh, s, hd), F32),
                        pltpu.VMEM((2 * nh, s, hd), F32)],
        compiler_params=_params(("parallel", "parallel")),
        name="dn_core",
    )(qn, kn, knt, vs, rows, etot, proj, norm_w.reshape(1, hd))


def _per_unit(a):
    b, t, _ = a.shape
    a = a.reshape(b, t, 2, DN_QK_HEADS, 2)
    return a.transpose(0, 3, 2, 4, 1).reshape(b, DN_QK_HEADS, N_UNITS, t)


def _deltanet_mixer(proj, ba, conv_w, a_log, dt_bias, norm_w, gate_col):
    b, s, _ = proj.shape
    npair = s // DN_PAIR
    nchunk = s // DN_CHUNK
    qn = _dn_prep(proj, conv_w, 0, DN_QK_WIDTH, True, DN_HEAD_DIM ** -0.5, False)
    kn, knt = _dn_prep(proj, conv_w, DN_QK_WIDTH, DN_QK_WIDTH, True, 1.0, True)
    vs = _dn_prep(proj, conv_w, 2 * DN_QK_WIDTH, DN_V_WIDTH, False, 1.0, False)
    g_cum, beta, g_tot, e_tot = _dn_gates(ba, a_log, dt_bias)
    gt_tok = jnp.broadcast_to(g_tot[:, :, None, :], (b, nchunk, DN_CHUNK, g_tot.shape[-1])).reshape(b, s, -1)
    rows = jnp.concatenate([_per_unit(g_cum), _per_unit(beta), _per_unit(gt_tok),
                            jnp.zeros((b, DN_QK_HEADS, N_UNITS, s), F32)], axis=2)
    rows = rows.reshape(b, DN_QK_HEADS, 16, npair, DN_PAIR).transpose(0, 1, 3, 2, 4)
    etot = _per_unit(e_tot).reshape(b, DN_QK_HEADS, N_UNITS * nchunk)
    etot = jnp.broadcast_to(etot[..., None], (b, DN_QK_HEADS, N_UNITS * nchunk, DN_HEAD_DIM))
    return _dn_core(qn, kn, knt, vs, rows, etot, proj, norm_w, gate_col)


def kernel(x, mem, norm_w, mem_norm_w, w_kv_mem, w_out, pool_w_in, pool_w_group, pool_scale,
           dn_w_in, dn_conv_w, dn_a_log, dn_dt_bias, dn_norm_w, final_norm_w):
    b, s, d = x.shape
    m = mem.shape[1]
    depth = norm_w.shape[0]
    x2d = x.reshape(b * s, d)
    mem2d = mem.reshape(b * m, d)
    tm = min(1024, b * s)
    for i in range(depth):
        j = i // 2
        kv = _norm_matmul(mem2d, mem_norm_w[i], w_kv_mem[i].astype(BF16), BF16, min(1024, b * m), 1024)
        kv = kv.reshape(b, m, 2 * XA_WIDTH)
        w1 = w_out[i][:POOL_WIDTH].astype(BF16)
        w2 = w_out[i][POOL_WIDTH:].astype(BF16)
        if i % 2 == 0:
            proj = _norm_matmul(x2d, norm_w[i], pool_w_in[j].astype(BF16), BF16, tm, 1024)
            proj = proj.reshape(b, s, -1)
            gate_col = POOL_WIDTH + XA_WIDTH
            ym = _pool_mix(proj, pool_w_group[j].astype(BF16), pool_scale[j], gate_col)
            ya = _xattn(proj, kv, POOL_WIDTH, gate_col + POOL_WIDTH, min(512, s))
        else:
            w_in = dn_w_in[j]
            proj = _norm_matmul(x2d, norm_w[i], w_in[:, :DN_MAIN].astype(BF16), BF16, tm, 1024)
            proj = proj.reshape(b, s, -1)
            ba = _norm_matmul(x2d, norm_w[i], w_in[:, DN_MAIN:].astype(BF16), F32, tm, 4 * DN_V_HEADS)
            ba = ba.reshape(b, s, -1)
            gate_col = DN_CONV_CH + XA_WIDTH
            ym = _deltanet_mixer(proj, ba, dn_conv_w[j], dn_a_log[j], dn_dt_bias[j], dn_norm_w[j], gate_col)
            ya = _xattn(proj, kv, DN_CONV_CH, gate_col + DN_V_WIDTH, min(512, s))
        x2d = _out_proj(ym.reshape(b * s, -1), ya.reshape(b * s, -1), w1, w2, x2d, min(512, b * s), 1024)
    return _rmsnorm(x2d, final_norm_w, min(512, b * s)).reshape(b, s, d)
```

```python
import functools

import jax
import jax.numpy as jnp
from jax import lax
from jax.experimental import pallas as pl
from jax.experimental.pallas import tpu as pltpu

F32 = jnp.float32
BF16 = jnp.bfloat16

EPS = 1e-6
D_MODEL = 2048
N_MEM = 256
XA_HEADS = 4
XA_HEAD_DIM = 512
XA_WIDTH = XA_HEADS * XA_HEAD_DIM
POOL_GROUPS = 4
POOL_GROUP_WIDTH = 1024
POOL_WIDTH = POOL_GROUPS * POOL_GROUP_WIDTH
DN_QK_HEADS = 16
DN_V_HEADS = 32
DN_HEAD_DIM = 128
DN_QK_WIDTH = DN_QK_HEADS * DN_HEAD_DIM
DN_V_WIDTH = DN_V_HEADS * DN_HEAD_DIM
DN_CONV_CH = 2 * DN_QK_WIDTH + DN_V_WIDTH
DN_CONV_TAPS = 5
DN_CHUNK = 64
DN_PAIR = 2 * DN_CHUNK
INNER = POOL_WIDTH + XA_WIDTH
DN_MAIN = DN_CONV_CH + XA_WIDTH + INNER

VMEM_LIMIT = 56 * 1024 * 1024


def _params(sem):
    return pltpu.CompilerParams(dimension_semantics=sem, vmem_limit_bytes=VMEM_LIMIT)


def _silu(g):
    return g * jax.nn.sigmoid(g)


def _rmsnorm_kernel(x_ref, nw_ref, o_ref):
    xf = x_ref[...]
    ms = jnp.mean(xf * xf, axis=-1, keepdims=True)
    o_ref[...] = (xf * lax.rsqrt(ms + EPS) * nw_ref[...]).astype(o_ref.dtype)


def _rmsnorm(x2d, nw, out_dtype, tm):
    m, k = x2d.shape
    return pl.pallas_call(
        _rmsnorm_kernel,
        grid=(m // tm,),
        in_specs=[pl.BlockSpec((tm, k), lambda i: (i, 0)),
                  pl.BlockSpec((1, k), lambda i: (0, 0))],
        out_specs=pl.BlockSpec((tm, k), lambda i: (i, 0)),
        out_shape=jax.ShapeDtypeStruct((m, k), out_dtype),
        compiler_params=_params(("parallel",)),
        name="rmsnorm",
    )(x2d, nw.reshape(1, k))


def _proj_kernel(h_ref, w_ref, o_ref, wb_ref):
    @pl.when(pl.program_id(1) == 0)
    def _():
        wb_ref[...] = w_ref[...].astype(BF16)

    o_ref[...] = jnp.dot(h_ref[...], wb_ref[...], preferred_element_type=F32).astype(o_ref.dtype)


def _proj(h2d, w_stack, layer, col0, ncols, out_dtype, tm, tn):
    m, k = h2d.shape
    cb = col0 // tn
    return pl.pallas_call(
        _proj_kernel,
        grid=(ncols // tn, m // tm),
        in_specs=[pl.BlockSpec((tm, k), lambda j, i: (i, 0)),
                  pl.BlockSpec((None, k, tn), lambda j, i: (layer, 0, cb + j))],
        out_specs=pl.BlockSpec((tm, tn), lambda j, i: (i, j)),
        out_shape=jax.ShapeDtypeStruct((m, ncols), out_dtype),
        scratch_shapes=[pltpu.VMEM((k, tn), BF16)],
        compiler_params=_params(("parallel", "arbitrary")),
        name="proj",
    )(h2d, w_stack)


def _out_proj_kernel(ym_ref, ya_ref, w1_ref, w2_ref, x_ref, o_ref, w1b_ref, w2b_ref):
    @pl.when(pl.program_id(1) == 0)
    def _():
        w1b_ref[...] = w1_ref[...].astype(BF16)
        w2b_ref[...] = w2_ref[...].astype(BF16)

    acc = jnp.dot(ym_ref[...], w1b_ref[...], preferred_element_type=F32)
    acc = acc + jnp.dot(ya_ref[...], w2b_ref[...], preferred_element_type=F32)
    o_ref[...] = x_ref[...] + acc


def _out_proj(ym, ya, w_stack, layer, x2d, tm, tn):
    m, n = x2d.shape
    k1, k2 = ym.shape[1], ya.shape[1]
    assert k1 % k2 == 0
    return pl.pallas_call(
        _out_proj_kernel,
        grid=(n // tn, m // tm),
        in_specs=[pl.BlockSpec((tm, k1), lambda j, i: (i, 0)),
                  pl.BlockSpec((tm, k2), lambda j, i: (i, 0)),
                  pl.BlockSpec((None, k1, tn), lambda j, i: (layer, 0, j)),
                  pl.BlockSpec((None, k2, tn), lambda j, i: (layer, k1 // k2, j)),
                  pl.BlockSpec((tm, tn), lambda j, i: (i, j))],
        out_specs=pl.BlockSpec((tm, tn), lambda j, i: (i, j)),
        out_shape=jax.ShapeDtypeStruct((m, n), F32),
        scratch_shapes=[pltpu.VMEM((k1, tn), BF16), pltpu.VMEM((k2, tn), BF16)],
        compiler_params=_params(("parallel", "arbitrary")),
        name="out_proj",
    )(ym, ya, w_stack, w_stack, x2d)


def _xattn_kernel(q_ref, k_ref, v_ref, g_ref, o_ref):
    s = lax.dot_general(q_ref[0], k_ref[0], (((1,), (1,)), ((), ())), preferred_element_type=F32)
    s = s * (XA_HEAD_DIM ** -0.5)
    p = jnp.exp(s - jnp.max(s, axis=-1, keepdims=True))
    l = jnp.sum(p, axis=-1, keepdims=True)
    o = jnp.dot(p.astype(BF16), v_ref[0], preferred_element_type=F32) / l
    o_ref[0] = (o * _silu(g_ref[0].astype(F32))).astype(o_ref.dtype)


def _xattn(proj, kv, q_col, gate_col, ts):
    b, s, _ = proj.shape
    hd = XA_HEAD_DIM
    qb, gb = q_col // hd, gate_col // hd
    return pl.pallas_call(
        _xattn_kernel,
        grid=(b, XA_HEADS, s // ts),
        in_specs=[pl.BlockSpec((1, ts, hd), lambda bi, h, t: (bi, t, qb + h)),
                  pl.BlockSpec((1, N_MEM, hd), lambda bi, h, t: (bi, 0, h)),
                  pl.BlockSpec((1, N_MEM, hd), lambda bi, h, t: (bi, 0, XA_HEADS + h)),
                  pl.BlockSpec((1, ts, hd), lambda bi, h, t: (bi, t, gb + h))],
        out_specs=pl.BlockSpec((1, ts, hd), lambda bi, h, t: (bi, t, h)),
        out_shape=jax.ShapeDtypeStruct((b, s, XA_WIDTH), BF16),
        compiler_params=_params(("parallel", "parallel", "parallel")),
        name="mem_xattn",
    )(proj, kv, kv, proj)


POOL_TILE = 128
POOL_SUBTILES = 2
POOL_HALO = 64


def _pool_kernel(u_ref, g_ref, wg_ref, sc_ref, o_ref, ext_ref):
    s = u_ref.shape[1]
    gw = u_ref.shape[2]
    grp = pl.program_id(1)
    half = jnp.left_shift(1, grp)
    ext_ref[0:POOL_HALO, :] = jnp.zeros((POOL_HALO, gw), BF16)
    ext_ref[POOL_HALO + s:POOL_HALO + s + POOL_HALO, :] = jnp.zeros((POOL_HALO, gw), BF16)
    ext_ref[POOL_HALO:POOL_HALO + s, :] = u_ref[0]

    kk = POOL_TILE + 2 * POOL_HALO
    r = lax.broadcasted_iota(jnp.int32, (POOL_TILE, kk), 0)
    c = lax.broadcasted_iota(jnp.int32, (POOL_TILE, kk), 1)
    d = c - POOL_HALO - r
    band = jnp.where((d >= -half) & (d < half), 1.0, 0.0).astype(BF16)
    wg = wg_ref[0]
    scale = sc_ref[0]

    rows = POOL_SUBTILES * POOL_TILE

    def body(i, carry):
        base = pl.multiple_of(i * rows, rows)
        pooled = []
        for sub in range(POOL_SUBTILES):
            t0 = base + sub * POOL_TILE
            wsum = jnp.dot(band, ext_ref[pl.ds(t0, kk), :], preferred_element_type=F32)
            tok = t0 + lax.broadcasted_iota(jnp.int32, (POOL_TILE, gw), 0)
            cnt = (jnp.minimum(tok + half, s) - jnp.maximum(tok - half, 0)).astype(F32)
            centre = ext_ref[pl.ds(t0 + POOL_HALO, POOL_TILE), :].astype(F32)
            pooled.append((wsum / cnt - centre).astype(BF16))
        mixed = jnp.dot(jnp.concatenate(pooled, axis=0), wg, preferred_element_type=F32) * scale
        gate = g_ref[0, pl.ds(base, rows), :].astype(F32)
        o_ref[0, pl.ds(base, rows), :] = (mixed * _silu(gate)).astype(o_ref.dtype)
        return carry

    lax.fori_loop(0, s // rows, body, 0)


def _pool_mix(proj, wg, scale, gate_col):
    b, s, _ = proj.shape
    gw = POOL_GROUP_WIDTH
    gb = gate_col // gw
    return pl.pallas_call(
        _pool_kernel,
        grid=(b, POOL_GROUPS),
        in_specs=[pl.BlockSpec((1, s, gw), lambda bi, g: (bi, 0, g)),
                  pl.BlockSpec((1, s, gw), lambda bi, g: (bi, 0, gb + g)),
                  pl.BlockSpec((1, gw, gw), lambda bi, g: (g, 0, 0)),
                  pl.BlockSpec((1, 1, gw), lambda bi, g: (g, 0, 0))],
        out_specs=pl.BlockSpec((1, s, gw), lambda bi, g: (bi, 0, g)),
        out_shape=jax.ShapeDtypeStruct((b, s, POOL_WIDTH), BF16),
        scratch_shapes=[pltpu.VMEM((s + 2 * POOL_HALO, gw), BF16)],
        compiler_params=_params(("parallel", "parallel")),
        name="pool_mix",
    )(proj, proj, wg, scale.reshape(POOL_GROUPS, 1, gw))


CONV_PAD = 8


def _dn_prep_kernel(x_ref, cw_ref, *refs, normalize, scale, emit_t):
    if emit_t:
        o_ref, ot_ref, ext_ref = refs
    else:
        o_ref, ext_ref = refs
    s = x_ref.shape[1]
    hd = x_ref.shape[2]
    ext_ref[0:CONV_PAD, :] = jnp.zeros((CONV_PAD, hd), F32)
    ext_ref[CONV_PAD + s:2 * CONV_PAD + s, :] = jnp.zeros((CONV_PAD, hd), F32)
    ext_ref[CONV_PAD:CONV_PAD + s, :] = x_ref[0].astype(F32)
    cw = cw_ref[...]
    tile = min(s, 512 if normalize else 256)

    def body(i, carry):
        t0 = pl.multiple_of(i * tile, tile)
        acc = None
        for tap in range(DN_CONV_TAPS):
            off = CONV_PAD + tap - DN_CONV_TAPS // 2
            term = ext_ref[pl.ds(t0 + off, tile), :] * cw[tap:tap + 1, :]
            acc = term if acc is None else acc + term
        y = _silu(acc)
        if normalize:
            y = y * lax.rsqrt(jnp.sum(y * y, axis=-1, keepdims=True) + EPS)
            if scale != 1.0:
                y = y * scale
        o_ref[0, pl.ds(t0, tile), :] = y.astype(o_ref.dtype)
        if emit_t:
            for h in range(tile // DN_PAIR):
                ot_ref[0, 0, i * (tile // DN_PAIR) + h] = y[h * DN_PAIR:(h + 1) * DN_PAIR, :].T.astype(ot_ref.dtype)
        return carry

    lax.fori_loop(0, s // tile, body, 0)


def _dn_prep(proj, conv_w, col0, ncols, normalize, scale, emit_t):
    b, s, _ = proj.shape
    hd = DN_HEAD_DIM
    nh = ncols // hd
    cb = col0 // hd
    out_shape = [jax.ShapeDtypeStruct((b, s, ncols), BF16)]
    out_specs = [pl.BlockSpec((1, s, hd), lambda bi, h: (bi, 0, h))]
    if emit_t:
        out_shape.append(jax.ShapeDtypeStruct((b, nh, s // DN_PAIR, hd, DN_PAIR), BF16))
        out_specs.append(pl.BlockSpec((1, 1, s // DN_PAIR, hd, DN_PAIR), lambda bi, h: (bi, h, 0, 0, 0)))
    res = pl.pallas_call(
        functools.partial(_dn_prep_kernel, normalize=normalize, scale=scale, emit_t=emit_t),
        grid=(b, nh),
        in_specs=[pl.BlockSpec((1, s, hd), lambda bi, h: (bi, 0, cb + h)),
                  pl.BlockSpec((DN_CONV_TAPS, hd), lambda bi, h: (0, cb + h))],
        out_specs=out_specs,
        out_shape=out_shape,
        scratch_shapes=[pltpu.VMEM((s + 2 * CONV_PAD, hd), F32)],
        compiler_params=_params(("parallel", "parallel")),
        name="dn_conv",
    )(proj, conv_w)
    return res if emit_t else res[0]


def _dn_gates_kernel(ba_ref, alog_ref, dtb_ref, g_ref, beta_ref, gt_ref, et_ref):
    s = ba_ref.shape[1]
    nh2 = 2 * DN_V_HEADS
    ba = ba_ref[0]
    beta_ref[0] = jax.nn.sigmoid(ba[:, :nh2])
    z = ba[:, nh2:] + dtb_ref[...]
    softplus = jnp.maximum(z, 0.0) + jnp.log(1.0 + jnp.exp(-jnp.abs(z)))
    g = -jnp.exp(alog_ref[...]) * softplus
    r = lax.broadcasted_iota(jnp.int32, (DN_CHUNK, DN_CHUNK), 0)
    c = lax.broadcasted_iota(jnp.int32, (DN_CHUNK, DN_CHUNK), 1)
    lower = jnp.where(r >= c, 1.0, 0.0).astype(F32)
    upper = jnp.where(r <= c, 1.0, 0.0).astype(F32)
    is_fwd = lax.broadcasted_iota(jnp.int32, (DN_CHUNK, nh2), 1) < DN_V_HEADS
    for n in range(s // DN_CHUNK):
        gc = g[n * DN_CHUNK:(n + 1) * DN_CHUNK, :]
        pre = jnp.dot(lower, gc, preferred_element_type=F32, precision=lax.Precision.HIGHEST)
        suf = jnp.dot(upper, gc, preferred_element_type=F32, precision=lax.Precision.HIGHEST)
        g_ref[0, n * DN_CHUNK:(n + 1) * DN_CHUNK, :] = jnp.where(is_fwd, pre, suf)
        g_tot = jnp.sum(gc, axis=0, keepdims=True)
        gt_ref[0, n:n + 1, :] = g_tot
        et_ref[0, n:n + 1, :] = jnp.exp(g_tot)


def _dn_gates(ba, a_log, dt_bias):
    b, s, w = ba.shape
    nh2 = 2 * DN_V_HEADS
    nchunk = s // DN_CHUNK
    return pl.pallas_call(
        _dn_gates_kernel,
        grid=(b,),
        in_specs=[pl.BlockSpec((1, s, w), lambda bi: (bi, 0, 0)),
                  pl.BlockSpec((1, nh2), lambda bi: (0, 0)),
                  pl.BlockSpec((1, nh2), lambda bi: (0, 0))],
        out_specs=[pl.BlockSpec((1, s, nh2), lambda bi: (bi, 0, 0)),
                   pl.BlockSpec((1, s, nh2), lambda bi: (bi, 0, 0)),
                   pl.BlockSpec((1, nchunk, nh2), lambda bi: (bi, 0, 0)),
                   pl.BlockSpec((1, nchunk, nh2), lambda bi: (bi, 0, 0))],
        out_shape=[jax.ShapeDtypeStruct((b, s, nh2), F32),
                   jax.ShapeDtypeStruct((b, s, nh2), F32),
                   jax.ShapeDtypeStruct((b, nchunk, nh2), F32),
                   jax.ShapeDtypeStruct((b, nchunk, nh2), F32)],
        compiler_params=_params(("parallel",)),
        name="dn_gates",
    )(ba, a_log.reshape(1, nh2), dt_bias.reshape(1, nh2))


N_UNITS = 4
DN_HEADS_PER_STEP = 2
DN_PAIRS_PER_ITER = 2
ROW_G, ROW_BETA, ROW_GT = 0, 4, 8


def _dn_core_kernel(q_ref, k_ref, kt_ref, v_ref, r_ref, et_ref, gate_ref, nw_ref, o_ref,
                    u_sc, w_sc, a_sc, qd_sc, kd_sc, st_ref, of_ref, ob_ref):
    s = q_ref.shape[1]
    hd = DN_HEAD_DIM
    pr = DN_PAIR
    npair = s // pr
    nchunk = 2 * npair

    row = lax.broadcasted_iota(jnp.int32, (pr, pr), 0)
    col = lax.broadcasted_iota(jnp.int32, (pr, pr), 1)
    same = jnp.right_shift(row, 6) == jnp.right_shift(col, 6)
    eye = jnp.where(row == col, 1.0, 0.0).astype(F32)
    masks = {False: (same & (row >= col), same & (row > col)),
             True: (same & (row <= col), same & (row < col))}
    zeros_half = jnp.zeros((DN_CHUNK, hd), F32)

    def mm(a, b):
        return jnp.dot(a, b, preferred_element_type=F32)


    nheads = DN_HEADS_PER_STEP

    def phase1(it, carry):
        heads = []
        for sp in range(DN_PAIRS_PER_ITER):
            pi = it * DN_PAIRS_PER_ITER + sp
            t0 = pl.multiple_of(pi * pr, pr)
            for hl in range(nheads):
                heads.append((hl, pi, q_ref[0, pl.ds(t0, pr), hl * hd:(hl + 1) * hd],
                              k_ref[0, pl.ds(t0, pr), hl * hd:(hl + 1) * hd], kt_ref[0, hl, pi],
                              v_ref[0, pl.ds(t0, pr), 2 * hl * hd:2 * (hl + 1) * hd], r_ref[0, hl, pi]))
        both = [mm(jnp.concatenate([q2, k2], axis=0), kt2) for _, _, q2, k2, kt2, _, _ in heads]
        chains = []
        for (hl, pi, q2, k2, kt2, v2, rows), qkk in zip(heads, both):
            qk, kk = qkk[:pr], qkk[pr:]
            q2f, k2f, kt2f = q2.astype(F32), k2.astype(F32), kt2.astype(F32)
            for sub in range(N_UNITS):
                unit = hl * N_UNITS + sub
                incl, strict = masks[sub >= 2]
                g_row = rows[ROW_G + sub:ROW_G + sub + 1]
                b_row = rows[ROW_BETA + sub:ROW_BETA + sub + 1]
                gt_row = rows[ROW_GT + sub:ROW_GT + sub + 1]
                m2 = jnp.broadcast_to(g_row, (pr, pr))
                m1 = m2.T
                dm = jnp.where(incl, jnp.exp(jnp.where(incl, m1 - m2, 0.0)), 0.0)
                kt_mat = jnp.where(strict, kk * dm, 0.0) * b_row
                e1 = jnp.exp(m1)
                vh = sub % 2
                a_sc[unit, pi] = (qk * dm * b_row).astype(BF16)
                qd_sc[unit, pi] = (q2f * e1).astype(BF16)
                kd_sc[unit, pi] = (kt2f * (b_row * jnp.exp(gt_row - g_row))).astype(BF16)
                chains.append(dict(
                    unit=unit, pi=pi, x=eye - kt_mat, nb=(-kt_mat).astype(BF16),
                    rhs=jnp.concatenate([v2[:, vh * hd:(vh + 1) * hd], (k2f * e1).astype(BF16)], axis=1)))
        for ch, n2 in zip(chains, [mm(ch["nb"], ch["nb"]) for ch in chains]):
            ch["pb"] = n2.astype(BF16)
        for _ in range(4):
            rrs = [mm(ch["pb"], jnp.concatenate([ch["pb"], ch["x"].astype(BF16)], axis=1)) for ch in chains]
            for ch, rr in zip(chains, rrs):
                ch["x"] = ch["x"] + rr[:, pr:]
                ch["pb"] = rr[:, :pr].astype(BF16)
        for ch, last in zip(chains, [mm(ch["pb"], ch["x"].astype(BF16)) for ch in chains]):
            ch["x"] = ch["x"] + last
        sols = [mm(ch["x"].astype(BF16), ch["rhs"]) for ch in chains]
        for ch, sol in zip(chains, sols):
            unit, pi = ch["unit"], ch["pi"]
            u_sc[unit, pi] = sol[:, :hd]
            w_sc[unit, pi] = sol[:, hd:].astype(BF16)
        return carry

    lax.fori_loop(0, npair // DN_PAIRS_PER_ITER, phase1, 0)

    st_ref[...] = jnp.zeros(st_ref.shape, F32)

    def phase2(it, carry):
        units = []
        for unit in range(nheads * N_UNITS):
            hl, sub = divmod(unit, N_UNITS)
            backward = sub >= 2
            pi = (npair - 1 - it) if backward else it
            units.append(dict(
                unit=unit, backward=backward, pi=pi, vhead=2 * hl + sub % 2,
                u=u_sc[unit, pi], w=w_sc[unit, pi], a=a_sc[unit, pi],
                qd=qd_sc[unit, pi], kd=kd_sc[unit, pi], st=st_ref[unit], outs=[None, None],
                decay=[et_ref[0, hl, pl.ds(sub * nchunk + 2 * pi + c, 1), :] for c in range(2)]))
        for step in range(2):
            for un in units:
                c = (1 - step) if un["backward"] else step
                un["c"] = c
                un["rows"] = slice(c * DN_CHUNK, (c + 1) * DN_CHUNK)
            rss = [mm(jnp.concatenate([un["w"][un["rows"]], un["qd"][un["rows"]]], axis=0),
                      un["st"].astype(BF16)) for un in units]
            for un, rs in zip(units, rss):
                halves = [zeros_half, zeros_half]
                halves[un["c"]] = un["u"][un["rows"]] - rs[:DN_CHUNK]
                un["v_full"] = jnp.concatenate(halves, axis=0).astype(BF16)
                un["qs"] = rs[DN_CHUNK:]
            intra = [mm(un["a"][un["rows"]], un["v_full"]) for un in units]
            upd = [mm(un["kd"], un["v_full"]) for un in units]
            for un, o_in, st_up in zip(units, intra, upd):
                un["outs"][un["c"]] = un["qs"] + o_in
                un["st"] = un["st"] * un["decay"][un["c"]] + st_up
        for un in units:
            t0 = pl.multiple_of(un["pi"] * pr, pr)
            dst = ob_ref if un["backward"] else of_ref
            dst[un["vhead"], pl.ds(t0, pr), :] = jnp.concatenate(un["outs"], axis=0)
            st_ref[un["unit"]] = un["st"]
        return carry

    lax.fori_loop(0, npair, phase2, 0)

    tile = 256
    nw = nw_ref[...]

    def epilogue(i, carry):
        t0 = pl.multiple_of(i * tile, tile)
        for vh in range(2 * nheads):
            o = of_ref[vh, pl.ds(t0, tile), :] + ob_ref[vh, pl.ds(t0, tile), :]
            o = o * lax.rsqrt(jnp.mean(o * o, axis=-1, keepdims=True) + EPS) * nw
            gate = gate_ref[0, pl.ds(t0, tile), vh * hd:(vh + 1) * hd].astype(F32)
            o_ref[0, pl.ds(t0, tile), vh * hd:(vh + 1) * hd] = (o * _silu(gate)).astype(o_ref.dtype)
        return carry

    lax.fori_loop(0, s // tile, epilogue, 0)


def _dn_core(qn, kn, knt, vs, rows, etot, proj, norm_w, gate_col):
    b, s, _ = qn.shape
    hd = DN_HEAD_DIM
    npair = s // DN_PAIR
    nh = DN_HEADS_PER_STEP
    nu = nh * N_UNITS
    gb = gate_col // (2 * nh * hd)
    return pl.pallas_call(
        _dn_core_kernel,
        grid=(b, DN_QK_HEADS // nh),
        in_specs=[pl.BlockSpec((1, s, nh * hd), lambda bi, j: (bi, 0, j)),
                  pl.BlockSpec((1, s, nh * hd), lambda bi, j: (bi, 0, j)),
                  pl.BlockSpec((1, nh, npair, hd, DN_PAIR), lambda bi, j: (bi, j, 0, 0, 0)),
                  pl.BlockSpec((1, s, 2 * nh * hd), lambda bi, j: (bi, 0, j)),
                  pl.BlockSpec((1, nh, npair, 16, DN_PAIR), lambda bi, j: (bi, j, 0, 0, 0)),
                  pl.BlockSpec((1, nh, N_UNITS * 2 * npair, hd), lambda bi, j: (bi, j, 0, 0)),
                  pl.BlockSpec((1, s, 2 * nh * hd), lambda bi, j: (bi, 0, gb + j)),
                  pl.BlockSpec((1, hd), lambda bi, j: (0, 0))],
        out_specs=pl.BlockSpec((1, s, 2 * nh * hd), lambda bi, j: (bi, 0, j)),
        out_shape=jax.ShapeDtypeStruct((b, s, DN_V_WIDTH), BF16),
        scratch_shapes=[pltpu.VMEM((nu, npair, DN_PAIR, hd), F32),
                        pltpu.VMEM((nu, npair, DN_PAIR, hd), BF16),
                        pltpu.VMEM((nu, npair, DN_PAIR, DN_PAIR), BF16),
                        pltpu.VMEM((nu, npair, DN_PAIR, hd), BF16),
                        pltpu.VMEM((nu, npair, hd, DN_PAIR), BF16),
                        pltpu.VMEM((nu, hd, hd), F32),
                        pltpu.VMEM((2 * nh, s, hd), F32),
                        pltpu.VMEM((2 * nh, s, hd), F32)],
        compiler_params=_params(("parallel", "parallel")),
        name="dn_core",
    )(qn, kn, knt, vs, rows, etot, proj, norm_w.reshape(1, hd))


def _per_unit(a):
    b, t, _ = a.shape
    a = a.reshape(b, t, 2, DN_QK_HEADS, 2)
    return a.transpose(0, 3, 2, 4, 1).reshape(b, DN_QK_HEADS, N_UNITS, t)


def _deltanet_mixer(proj, ba, conv_w, a_log, dt_bias, norm_w, gate_col):
    b, s, _ = proj.shape
    npair = s // DN_PAIR
    nchunk = s // DN_CHUNK
    qn = _dn_prep(proj, conv_w, 0, DN_QK_WIDTH, True, DN_HEAD_DIM ** -0.5, False)
    kn, knt = _dn_prep(proj, conv_w, DN_QK_WIDTH, DN_QK_WIDTH, True, 1.0, True)
    vs = _dn_prep(proj, conv_w, 2 * DN_QK_WIDTH, DN_V_WIDTH, False, 1.0, False)
    g_cum, beta, g_tot, e_tot = _dn_gates(ba, a_log, dt_bias)
    gt_tok = jnp.broadcast_to(g_tot[:, :, None, :], (b, nchunk, DN_CHUNK, g_tot.shape[-1])).reshape(b, s, -1)
    rows = jnp.concatenate([_per_unit(g_cum), _per_unit(beta), _per_unit(gt_tok),
                            jnp.zeros((b, DN_QK_HEADS, N_UNITS, s), F32)], axis=2)
    rows = rows.reshape(b, DN_QK_HEADS, 16, npair, DN_PAIR).transpose(0, 1, 3, 2, 4)
    etot = _per_unit(e_tot).reshape(b, DN_QK_HEADS, N_UNITS * nchunk)
    etot = jnp.broadcast_to(etot[..., None], (b, DN_QK_HEADS, N_UNITS * nchunk, DN_HEAD_DIM))
    return _dn_core(qn, kn, knt, vs, rows, etot, proj, norm_w, gate_col)


def kernel(x, mem, norm_w, mem_norm_w, w_kv_mem, w_out, pool_w_in, pool_w_group, pool_scale,
           dn_w_in, dn_conv_w, dn_a_log, dn_dt_bias, dn_norm_w, final_norm_w):
    b, s, d = x.shape
    m = mem.shape[1]
    depth = norm_w.shape[0]
    x2d = x.reshape(b * s, d)
    mem2d = mem.reshape(b * m, d)
    tm = min(1024, b * s)
    tr = min(512, b * s)
    for i in range(depth):
        j = i // 2
        hm = _rmsnorm(mem2d, mem_norm_w[i], BF16, min(512, b * m))
        kv = _proj(hm, w_kv_mem, i, 0, 2 * XA_WIDTH, BF16, min(1024, b * m), 1024)
        kv = kv.reshape(b, m, 2 * XA_WIDTH)
        h = _rmsnorm(x2d, norm_w[i], BF16, tr)
        if i % 2 == 0:
            proj = _proj(h, pool_w_in, j, 0, pool_w_in.shape[2], BF16, tm, 1024)
            proj = proj.reshape(b, s, -1)
            gate_col = POOL_WIDTH + XA_WIDTH
            ym = _pool_mix(proj, pool_w_group[j].astype(BF16), pool_scale[j], gate_col)
            ya = _xattn(proj, kv, POOL_WIDTH, gate_col + POOL_WIDTH, min(512, s))
        else:
            proj = _proj(h, dn_w_in, j, 0, DN_MAIN, BF16, tm, 1024)
            proj = proj.reshape(b, s, -1)
            ba = _proj(h, dn_w_in, j, DN_MAIN, 4 * DN_V_HEADS, F32, tm, 4 * DN_V_HEADS)
            ba = ba.reshape(b, s, -1)
            gate_col = DN_CONV_CH + XA_WIDTH
            ym = _deltanet_mixer(proj, ba, dn_conv_w[j], dn_a_log[j], dn_dt_bias[j], dn_norm_w[j], gate_col)
            ya = _xattn(proj, kv, DN_CONV_CH, gate_col + DN_V_WIDTH, min(512, s))
        x2d = _out_proj(ym.reshape(b * s, -1), ya.reshape(b * s, -1), w_out, i, x2d, tr, 512)
    return _rmsnorm(x2d, final_norm_w, F32, tr).reshape(b, s, d)
```

```python
import functools

import jax
import jax.numpy as jnp
from jax import lax
from jax.experimental import pallas as pl
from jax.experimental.pallas import tpu as pltpu

F32 = jnp.float32
BF16 = jnp.bfloat16

EPS = 1e-6
D_MODEL = 2048
N_MEM = 256
XA_HEADS = 4
XA_HEAD_DIM = 512
XA_WIDTH = XA_HEADS * XA_HEAD_DIM
POOL_GROUPS = 4
POOL_GROUP_WIDTH = 1024
POOL_WIDTH = POOL_GROUPS * POOL_GROUP_WIDTH
DN_QK_HEADS = 16
DN_V_HEADS = 32
DN_HEAD_DIM = 128
DN_QK_WIDTH = DN_QK_HEADS * DN_HEAD_DIM
DN_V_WIDTH = DN_V_HEADS * DN_HEAD_DIM
DN_CONV_CH = 2 * DN_QK_WIDTH + DN_V_WIDTH
DN_CONV_TAPS = 5
DN_CHUNK = 64
DN_PAIR = 2 * DN_CHUNK
INNER = POOL_WIDTH + XA_WIDTH
DN_MAIN = DN_CONV_CH + XA_WIDTH + INNER

VMEM_LIMIT = 56 * 1024 * 1024


def _params(sem):
    return pltpu.CompilerParams(dimension_semantics=sem, vmem_limit_bytes=VMEM_LIMIT)


def _silu(g):
    return g * jax.nn.sigmoid(g)


def _rmsnorm_kernel(x_ref, nw_ref, o_ref):
    xf = x_ref[...]
    ms = jnp.mean(xf * xf, axis=-1, keepdims=True)
    o_ref[...] = (xf * lax.rsqrt(ms + EPS) * nw_ref[...]).astype(o_ref.dtype)


def _rmsnorm(x2d, nw, out_dtype, tm):
    m, k = x2d.shape
    return pl.pallas_call(
        _rmsnorm_kernel,
        grid=(m // tm,),
        in_specs=[pl.BlockSpec((tm, k), lambda i: (i, 0)),
                  pl.BlockSpec((1, k), lambda i: (0, 0))],
        out_specs=pl.BlockSpec((tm, k), lambda i: (i, 0)),
        out_shape=jax.ShapeDtypeStruct((m, k), out_dtype),
        compiler_params=_params(("parallel",)),
        name="rmsnorm",
    )(x2d, nw.reshape(1, k))


def _proj_kernel(h_ref, w_ref, o_ref, wb_ref):
    @pl.when(pl.program_id(1) == 0)
    def _():
        wb_ref[...] = w_ref[...].astype(BF16)

    o_ref[...] = jnp.dot(h_ref[...], wb_ref[...], preferred_element_type=F32).astype(o_ref.dtype)


def _proj(h2d, w_stack, layer, col0, ncols, out_dtype, tm, tn):
    m, k = h2d.shape
    cb = col0 // tn
    return pl.pallas_call(
        _proj_kernel,
        grid=(ncols // tn, m // tm),
        in_specs=[pl.BlockSpec((tm, k), lambda j, i: (i, 0)),
                  pl.BlockSpec((None, k, tn), lambda j, i: (layer, 0, cb + j))],
        out_specs=pl.BlockSpec((tm, tn), lambda j, i: (i, j)),
        out_shape=jax.ShapeDtypeStruct((m, ncols), out_dtype),
        scratch_shapes=[pltpu.VMEM((k, tn), BF16)],
        compiler_params=_params(("parallel", "arbitrary")),
        name="proj",
    )(h2d, w_stack)


def _out_proj_kernel(ym_ref, ya_ref, w1_ref, w2_ref, x_ref, o_ref, w1b_ref, w2b_ref):
    @pl.when(pl.program_id(1) == 0)
    def _():
        w1b_ref[...] = w1_ref[...].astype(BF16)
        w2b_ref[...] = w2_ref[...].astype(BF16)

    acc = jnp.dot(ym_ref[...], w1b_ref[...], preferred_element_type=F32)
    acc = acc + jnp.dot(ya_ref[...], w2b_ref[...], preferred_element_type=F32)
    o_ref[...] = x_ref[...] + acc


def _out_proj(ym, ya, w_stack, layer, x2d, tm, tn):
    m, n = x2d.shape
    k1, k2 = ym.shape[1], ya.shape[1]
    assert k1 % k2 == 0
    return pl.pallas_call(
        _out_proj_kernel,
        grid=(n // tn, m // tm),
        in_specs=[pl.BlockSpec((tm, k1), lambda j, i: (i, 0)),
                  pl.BlockSpec((tm, k2), lambda j, i: (i, 0)),
                  pl.BlockSpec((None, k1, tn), lambda j, i: (layer, 0, j), pipeline_mode=pl.Buffered(1)),
                  pl.BlockSpec((None, k2, tn), lambda j, i: (layer, k1 // k2, j), pipeline_mode=pl.Buffered(1)),
                  pl.BlockSpec((tm, tn), lambda j, i: (i, j))],
        out_specs=pl.BlockSpec((tm, tn), lambda j, i: (i, j)),
        out_shape=jax.ShapeDtypeStruct((m, n), F32),
        scratch_shapes=[pltpu.VMEM((k1, tn), BF16), pltpu.VMEM((k2, tn), BF16)],
        compiler_params=_params(("parallel", "arbitrary")),
        name="out_proj",
    )(ym, ya, w_stack, w_stack, x2d)


def _xattn_kernel(q_ref, k_ref, v_ref, g_ref, o_ref):
    s = lax.dot_general(q_ref[0], k_ref[0], (((1,), (1,)), ((), ())), preferred_element_type=F32)
    s = s * (XA_HEAD_DIM ** -0.5)
    p = jnp.exp(s - jnp.max(s, axis=-1, keepdims=True))
    l = jnp.sum(p, axis=-1, keepdims=True)
    o = jnp.dot(p.astype(BF16), v_ref[0], preferred_element_type=F32) / l
    o_ref[0] = (o * _silu(g_ref[0].astype(F32))).astype(o_ref.dtype)


def _xattn(proj, kv, q_col, gate_col, ts):
    b, s, _ = proj.shape
    hd = XA_HEAD_DIM
    qb, gb = q_col // hd, gate_col // hd
    return pl.pallas_call(
        _xattn_kernel,
        grid=(b, XA_HEADS, s // ts),
        in_specs=[pl.BlockSpec((1, ts, hd), lambda bi, h, t: (bi, t, qb + h)),
                  pl.BlockSpec((1, N_MEM, hd), lambda bi, h, t: (bi, 0, h)),
                  pl.BlockSpec((1, N_MEM, hd), lambda bi, h, t: (bi, 0, XA_HEADS + h)),
                  pl.BlockSpec((1, ts, hd), lambda bi, h, t: (bi, t, gb + h))],
        out_specs=pl.BlockSpec((1, ts, hd), lambda bi, h, t: (bi, t, h)),
        out_shape=jax.ShapeDtypeStruct((b, s, XA_WIDTH), BF16),
        compiler_params=_params(("parallel", "parallel", "parallel")),
        name="mem_xattn",
    )(proj, kv, kv, proj)


POOL_TILE = 128
POOL_SUBTILES = 2
POOL_HALO = 64


def _pool_kernel(u_ref, g_ref, wg_ref, sc_ref, o_ref, ext_ref):
    s = u_ref.shape[1]
    gw = u_ref.shape[2]
    grp = pl.program_id(1)
    half = jnp.left_shift(1, grp)
    ext_ref[0:POOL_HALO, :] = jnp.zeros((POOL_HALO, gw), BF16)
    ext_ref[POOL_HALO + s:POOL_HALO + s + POOL_HALO, :] = jnp.zeros((POOL_HALO, gw), BF16)
    ext_ref[POOL_HALO:POOL_HALO + s, :] = u_ref[0]

    kk = POOL_TILE + 2 * POOL_HALO
    r = lax.broadcasted_iota(jnp.int32, (POOL_TILE, kk), 0)
    c = lax.broadcasted_iota(jnp.int32, (POOL_TILE, kk), 1)
    d = c - POOL_HALO - r
    band = jnp.where((d >= -half) & (d < half), 1.0, 0.0).astype(BF16)
    wg = wg_ref[0]
    scale = sc_ref[0]

    rows = POOL_SUBTILES * POOL_TILE

    def body(i, carry):
        base = pl.multiple_of(i * rows, rows)
        pooled = []
        for sub in range(POOL_SUBTILES):
            t0 = base + sub * POOL_TILE
            wsum = jnp.dot(band, ext_ref[pl.ds(t0, kk), :], preferred_element_type=F32)
            tok = t0 + lax.broadcasted_iota(jnp.int32, (POOL_TILE, gw), 0)
            cnt = (jnp.minimum(tok + half, s) - jnp.maximum(tok - half, 0)).astype(F32)
            centre = ext_ref[pl.ds(t0 + POOL_HALO, POOL_TILE), :].astype(F32)
            pooled.append((wsum / cnt - centre).astype(BF16))
        mixed = jnp.dot(jnp.concatenate(pooled, axis=0), wg, preferred_element_type=F32) * scale
        gate = g_ref[0, pl.ds(base, rows), :].astype(F32)
        o_ref[0, pl.ds(base, rows), :] = (mixed * _silu(gate)).astype(o_ref.dtype)
        return carry

    lax.fori_loop(0, s // rows, body, 0)


def _pool_mix(proj, wg, scale, gate_col):
    b, s, _ = proj.shape
    gw = POOL_GROUP_WIDTH
    gb = gate_col // gw
    return pl.pallas_call(
        _pool_kernel,
        grid=(b, POOL_GROUPS),
        in_specs=[pl.BlockSpec((1, s, gw), lambda bi, g: (bi, 0, g)),
                  pl.BlockSpec((1, s, gw), lambda bi, g: (bi, 0, gb + g)),
                  pl.BlockSpec((1, gw, gw), lambda bi, g: (g, 0, 0)),
                  pl.BlockSpec((1, 1, gw), lambda bi, g: (g, 0, 0))],
        out_specs=pl.BlockSpec((1, s, gw), lambda bi, g: (bi, 0, g)),
        out_shape=jax.ShapeDtypeStruct((b, s, POOL_WIDTH), BF16),
        scratch_shapes=[pltpu.VMEM((s + 2 * POOL_HALO, gw), BF16)],
        compiler_params=_params(("parallel", "parallel")),
        name="pool_mix",
    )(proj, proj, wg, scale.reshape(POOL_GROUPS, 1, gw))


CONV_PAD = 8
CONV_HEADS_PER_STEP = 4


def _dn_prep_kernel(x_ref, cw_ref, *refs, normalize, scale, emit_t):
    if emit_t:
        o_ref, ot_ref, ext_ref = refs
    else:
        o_ref, ext_ref = refs
    s = x_ref.shape[1]
    width = x_ref.shape[2]
    hd = DN_HEAD_DIM
    for hl in range(width // hd):
        ext_ref[hl, 0:CONV_PAD, :] = jnp.zeros((CONV_PAD, hd), F32)
        ext_ref[hl, CONV_PAD + s:2 * CONV_PAD + s, :] = jnp.zeros((CONV_PAD, hd), F32)
        ext_ref[hl, CONV_PAD:CONV_PAD + s, :] = x_ref[0, :, hl * hd:(hl + 1) * hd].astype(F32)
    cw = cw_ref[...]
    tile = min(s, 256)

    def body(i, carry):
        t0 = pl.multiple_of(i * tile, tile)
        for hl in range(width // hd):
            lanes = slice(hl * hd, (hl + 1) * hd)
            acc = None
            for tap in range(DN_CONV_TAPS):
                off = CONV_PAD + tap - DN_CONV_TAPS // 2
                term = ext_ref[hl, pl.ds(t0 + off, tile), :] * cw[tap:tap + 1, lanes]
                acc = term if acc is None else acc + term
            y = _silu(acc)
            if normalize:
                y = y * lax.rsqrt(jnp.sum(y * y, axis=-1, keepdims=True) + EPS)
                if scale != 1.0:
                    y = y * scale
            o_ref[0, pl.ds(t0, tile), lanes] = y.astype(o_ref.dtype)
            if emit_t:
                for h in range(tile // DN_PAIR):
                    ot_ref[0, hl, i * (tile // DN_PAIR) + h] = (
                        y[h * DN_PAIR:(h + 1) * DN_PAIR, :].T.astype(ot_ref.dtype))
        return carry

    lax.fori_loop(0, s // tile, body, 0)


def _dn_prep(proj, conv_w, col0, ncols, normalize, scale, emit_t):
    b, s, _ = proj.shape
    hd = DN_HEAD_DIM
    width = CONV_HEADS_PER_STEP * hd
    nh = ncols // hd
    cb = col0 // width
    out_shape = [jax.ShapeDtypeStruct((b, s, ncols), BF16)]
    out_specs = [pl.BlockSpec((1, s, width), lambda bi, h: (bi, 0, h))]
    if emit_t:
        out_shape.append(jax.ShapeDtypeStruct((b, nh, s // DN_PAIR, hd, DN_PAIR), BF16))
        out_specs.append(pl.BlockSpec((1, CONV_HEADS_PER_STEP, s // DN_PAIR, hd, DN_PAIR),
                                      lambda bi, h: (bi, h, 0, 0, 0)))
    res = pl.pallas_call(
        functools.partial(_dn_prep_kernel, normalize=normalize, scale=scale, emit_t=emit_t),
        grid=(b, ncols // width),
        in_specs=[pl.BlockSpec((1, s, width), lambda bi, h: (bi, 0, cb + h)),
                  pl.BlockSpec((DN_CONV_TAPS, width), lambda bi, h: (0, cb + h))],
        out_specs=out_specs,
        out_shape=out_shape,
        scratch_shapes=[pltpu.VMEM((CONV_HEADS_PER_STEP, s + 2 * CONV_PAD, hd), F32)],
        compiler_params=_params(("parallel", "parallel")),
        name="dn_conv",
    )(proj, conv_w)
    return res if emit_t else res[0]


def _dn_gates_kernel(ba_ref, alog_ref, dtb_ref, g_ref, beta_ref, gt_ref, et_ref):
    s = ba_ref.shape[1]
    nh2 = 2 * DN_V_HEADS
    ba = ba_ref[0]
    beta_ref[0] = jax.nn.sigmoid(ba[:, :nh2])
    z = ba[:, nh2:] + dtb_ref[...]
    softplus = jnp.maximum(z, 0.0) + jnp.log(1.0 + jnp.exp(-jnp.abs(z)))
    g = -jnp.exp(alog_ref[...]) * softplus
    r = lax.broadcasted_iota(jnp.int32, (DN_CHUNK, DN_CHUNK), 0)
    c = lax.broadcasted_iota(jnp.int32, (DN_CHUNK, DN_CHUNK), 1)
    lower = jnp.where(r >= c, 1.0, 0.0).astype(F32)
    upper = jnp.where(r <= c, 1.0, 0.0).astype(F32)
    is_fwd = lax.broadcasted_iota(jnp.int32, (DN_CHUNK, nh2), 1) < DN_V_HEADS
    for n in range(s // DN_CHUNK):
        gc = g[n * DN_CHUNK:(n + 1) * DN_CHUNK, :]
        pre = jnp.dot(lower, gc, preferred_element_type=F32, precision=lax.Precision.HIGHEST)
        suf = jnp.dot(upper, gc, preferred_element_type=F32, precision=lax.Precision.HIGHEST)
        g_ref[0, n * DN_CHUNK:(n + 1) * DN_CHUNK, :] = jnp.where(is_fwd, pre, suf)
        g_tot = jnp.sum(gc, axis=0, keepdims=True)
        gt_ref[0, n:n + 1, :] = g_tot
        et_ref[0, n:n + 1, :] = jnp.exp(g_tot)


def _dn_gates(ba, a_log, dt_bias):
    b, s, w = ba.shape
    nh2 = 2 * DN_V_HEADS
    nchunk = s // DN_CHUNK
    return pl.pallas_call(
        _dn_gates_kernel,
        grid=(b,),
        in_specs=[pl.BlockSpec((1, s, w), lambda bi: (bi, 0, 0)),
                  pl.BlockSpec((1, nh2), lambda bi: (0, 0)),
                  pl.BlockSpec((1, nh2), lambda bi: (0, 0))],
        out_specs=[pl.BlockSpec((1, s, nh2), lambda bi: (bi, 0, 0)),
                   pl.BlockSpec((1, s, nh2), lambda bi: (bi, 0, 0)),
                   pl.BlockSpec((1, nchunk, nh2), lambda bi: (bi, 0, 0)),
                   pl.BlockSpec((1, nchunk, nh2), lambda bi: (bi, 0, 0))],
        out_shape=[jax.ShapeDtypeStruct((b, s, nh2), F32),
                   jax.ShapeDtypeStruct((b, s, nh2), F32),
                   jax.ShapeDtypeStruct((b, nchunk, nh2), F32),
                   jax.ShapeDtypeStruct((b, nchunk, nh2), F32)],
        compiler_params=_params(("parallel",)),
        name="dn_gates",
    )(ba, a_log.reshape(1, nh2), dt_bias.reshape(1, nh2))


N_UNITS = 4
DN_HEADS_PER_STEP = 2
DN_PAIRS_PER_ITER = 2
ROW_G, ROW_BETA, ROW_GT = 0, 4, 8


def _dn_core_kernel(q_ref, k_ref, kt_ref, v_ref, r_ref, et_ref, gate_ref, nw_ref, o_ref,
                    u_sc, w_sc, a_sc, qd_sc, kd_sc, st_ref, of_ref, ob_ref):
    s = q_ref.shape[1]
    hd = DN_HEAD_DIM
    pr = DN_PAIR
    npair = s // pr
    nchunk = 2 * npair

    row = lax.broadcasted_iota(jnp.int32, (pr, pr), 0)
    col = lax.broadcasted_iota(jnp.int32, (pr, pr), 1)
    same = jnp.right_shift(row, 6) == jnp.right_shift(col, 6)
    eye = jnp.where(row == col, 1.0, 0.0).astype(F32)
    masks = {False: (same & (row >= col), same & (row > col)),
             True: (same & (row <= col), same & (row < col))}
    zeros_half = jnp.zeros((DN_CHUNK, hd), F32)

    def mm(a, b):
        return jnp.dot(a, b, preferred_element_type=F32)


    nheads = DN_HEADS_PER_STEP

    def phase1(it, carry):
        heads = []
        for sp in range(DN_PAIRS_PER_ITER):
            pi = it * DN_PAIRS_PER_ITER + sp
            t0 = pl.multiple_of(pi * pr, pr)
            for hl in range(nheads):
                heads.append((hl, pi, q_ref[0, pl.ds(t0, pr), hl * hd:(hl + 1) * hd],
                              k_ref[0, pl.ds(t0, pr), hl * hd:(hl + 1) * hd], kt_ref[0, hl, pi],
                              v_ref[0, pl.ds(t0, pr), 2 * hl * hd:2 * (hl + 1) * hd], r_ref[0, hl, pi]))
        both = [mm(jnp.concatenate([q2, k2], axis=0), kt2) for _, _, q2, k2, kt2, _, _ in heads]
        chains = []
        for (hl, pi, q2, k2, kt2, v2, rows), qkk in zip(heads, both):
            qk, kk = qkk[:pr], qkk[pr:]
            q2f, k2f, kt2f = q2.astype(F32), k2.astype(F32), kt2.astype(F32)
            for sub in range(N_UNITS):
                unit = hl * N_UNITS + sub
                incl, strict = masks[sub >= 2]
                g_row = rows[ROW_G + sub:ROW_G + sub + 1]
                b_row = rows[ROW_BETA + sub:ROW_BETA + sub + 1]
                gt_row = rows[ROW_GT + sub:ROW_GT + sub + 1]
                m2 = jnp.broadcast_to(g_row, (pr, pr))
                m1 = m2.T
                dm = jnp.where(incl, jnp.exp(jnp.where(incl, m1 - m2, 0.0)), 0.0)
                kt_mat = jnp.where(strict, kk * dm, 0.0) * b_row
                e1 = jnp.exp(m1)
                vh = sub % 2
                a_sc[unit, pi] = (qk * dm * b_row).astype(BF16)
                qd_sc[unit, pi] = (q2f * e1).astype(BF16)
                kd_sc[unit, pi] = (kt2f * (b_row * jnp.exp(gt_row - g_row))).astype(BF16)
                chains.append(dict(
                    unit=unit, pi=pi, x=eye - kt_mat, nb=(-kt_mat).astype(BF16),
                    rhs=jnp.concatenate([v2[:, vh * hd:(vh + 1) * hd], (k2f * e1).astype(BF16)], axis=1)))
        for ch, n2 in zip(chains, [mm(ch["nb"], ch["nb"]) for ch in chains]):
            ch["pb"] = n2.astype(BF16)
        for _ in range(4):
            rrs = [mm(ch["pb"], jnp.concatenate([ch["pb"], ch["x"].astype(BF16)], axis=1)) for ch in chains]
            for ch, rr in zip(chains, rrs):
                ch["x"] = ch["x"] + rr[:, pr:]
                ch["pb"] = rr[:, :pr].astype(BF16)
        for ch, last in zip(chains, [mm(ch["pb"], ch["x"].astype(BF16)) for ch in chains]):
            ch["x"] = ch["x"] + last
        sols = [mm(ch["x"].astype(BF16), ch["rhs"]) for ch in chains]
        for ch, sol in zip(chains, sols):
            unit, pi = ch["unit"], ch["pi"]
            u_sc[unit, pi] = sol[:, :hd]
            w_sc[unit, pi] = sol[:, hd:].astype(BF16)
        return carry

    lax.fori_loop(0, npair // DN_PAIRS_PER_ITER, phase1, 0)

    st_ref[...] = jnp.zeros(st_ref.shape, F32)

    half = npair // 2
    nw = nw_ref[...]

    def phase2(it, carry, finalize):
        units = []
        for unit in range(nheads * N_UNITS):
            hl, sub = divmod(unit, N_UNITS)
            backward = sub >= 2
            pi = (npair - 1 - it) if backward else it
            units.append(dict(
                unit=unit, backward=backward, pi=pi, vhead=2 * hl + sub % 2,
                u=u_sc[unit, pi], w=w_sc[unit, pi], a=a_sc[unit, pi],
                qd=qd_sc[unit, pi], kd=kd_sc[unit, pi], st=st_ref[unit], outs=[None, None],
                decay=[et_ref[0, hl, pl.ds(sub * nchunk + 2 * pi + c, 1), :] for c in range(2)]))
        if finalize:
            for un in units:
                t0 = pl.multiple_of(un["pi"] * pr, pr)
                lanes = slice(un["vhead"] * hd, (un["vhead"] + 1) * hd)
                if un["backward"]:
                    un["other"] = of_ref[un["vhead"], pl.ds(t0, pr), :]
                else:
                    un["other"] = ob_ref[un["vhead"], pl.ds(pl.multiple_of((un["pi"] - half) * pr, pr), pr), :]
                un["gate"] = gate_ref[0, pl.ds(t0, pr), lanes]
        for step in range(2):
            for un in units:
                c = (1 - step) if un["backward"] else step
                un["c"] = c
                un["rows"] = slice(c * DN_CHUNK, (c + 1) * DN_CHUNK)
            rss = [mm(jnp.concatenate([un["w"][un["rows"]], un["qd"][un["rows"]]], axis=0),
                      un["st"].astype(BF16)) for un in units]
            for un, rs in zip(units, rss):
                halves = [zeros_half, zeros_half]
                halves[un["c"]] = un["u"][un["rows"]] - rs[:DN_CHUNK]
                un["v_full"] = jnp.concatenate(halves, axis=0).astype(BF16)
                un["qs"] = rs[DN_CHUNK:]
            intra = [mm(un["a"][un["rows"]], un["v_full"]) for un in units]
            upd = [mm(un["kd"], un["v_full"]) for un in units]
            for un, o_in, st_up in zip(units, intra, upd):
                un["outs"][un["c"]] = un["qs"] + o_in
                un["st"] = un["st"] * un["decay"][un["c"]] + st_up
        for un in units:
            t0 = pl.multiple_of(un["pi"] * pr, pr)
            o_pair = jnp.concatenate(un["outs"], axis=0)
            if finalize:
                o = o_pair + un["other"]
                o = o * lax.rsqrt(jnp.mean(o * o, axis=-1, keepdims=True) + EPS) * nw
                lanes = slice(un["vhead"] * hd, (un["vhead"] + 1) * hd)
                o_ref[0, pl.ds(t0, pr), lanes] = (o * _silu(un["gate"].astype(F32))).astype(o_ref.dtype)
            elif un["backward"]:
                ob_ref[un["vhead"], pl.ds(pl.multiple_of((un["pi"] - half) * pr, pr), pr), :] = o_pair
            else:
                of_ref[un["vhead"], pl.ds(t0, pr), :] = o_pair
            st_ref[un["unit"]] = un["st"]
        return carry

    lax.fori_loop(0, half, functools.partial(phase2, finalize=False), 0)
    lax.fori_loop(half, npair, functools.partial(phase2, finalize=True), 0)


def _dn_core(qn, kn, knt, vs, rows, etot, proj, norm_w, gate_col):
    b, s, _ = qn.shape
    hd = DN_HEAD_DIM
    npair = s // DN_PAIR
    nh = DN_HEADS_PER_STEP
    nu = nh * N_UNITS
    gb = gate_col // (2 * nh * hd)
    return pl.pallas_call(
        _dn_core_kernel,
        grid=(b, DN_QK_HEADS // nh),
        in_specs=[pl.BlockSpec((1, s, nh * hd), lambda bi, j: (bi, 0, j)),
                  pl.BlockSpec((1, s, nh * hd), lambda bi, j: (bi, 0, j)),
                  pl.BlockSpec((1, nh, npair, hd, DN_PAIR), lambda bi, j: (bi, j, 0, 0, 0)),
                  pl.BlockSpec((1, s, 2 * nh * hd), lambda bi, j: (bi, 0, j)),
                  pl.BlockSpec((1, nh, npair, 16, DN_PAIR), lambda bi, j: (bi, j, 0, 0, 0)),
                  pl.BlockSpec((1, nh, N_UNITS * 2 * npair, hd), lambda bi, j: (bi, j, 0, 0)),
                  pl.BlockSpec((1, s, 2 * nh * hd), lambda bi, j: (bi, 0, gb + j)),
                  pl.BlockSpec((1, hd), lambda bi, j: (0, 0))],
        out_specs=pl.BlockSpec((1, s, 2 * nh * hd), lambda bi, j: (bi, 0, j)),
        out_shape=jax.ShapeDtypeStruct((b, s, DN_V_WIDTH), BF16),
        scratch_shapes=[pltpu.VMEM((nu, npair, DN_PAIR, hd), F32),
                        pltpu.VMEM((nu, npair, DN_PAIR, hd), BF16),
                        pltpu.VMEM((nu, npair, DN_PAIR, DN_PAIR), BF16),
                        pltpu.VMEM((nu, npair, DN_PAIR, hd), BF16),
                        pltpu.VMEM((nu, npair, hd, DN_PAIR), BF16),
                        pltpu.VMEM((nu, hd, hd), F32),
                        pltpu.VMEM((2 * nh, s // 2, hd), F32),
                        pltpu.VMEM((2 * nh, s // 2, hd), F32)],
        compiler_params=_params(("parallel", "parallel")),
        name="dn_core",
    )(qn, kn, knt, vs, rows, etot, proj, norm_w.reshape(1, hd))


def _per_unit(a):
    b, t, _ = a.shape
    a = a.reshape(b, t, 2, DN_QK_HEADS, 2)
    return a.transpose(0, 3, 2, 4, 1).reshape(b, DN_QK_HEADS, N_UNITS, t)


def _deltanet_mixer(proj, ba, conv_w, a_log, dt_bias, norm_w, gate_col):
    b, s, _ = proj.shape
    npair = s // DN_PAIR
    nchunk = s // DN_CHUNK
    qn = _dn_prep(proj, conv_w, 0, DN_QK_WIDTH, True, DN_HEAD_DIM ** -0.5, False)
    kn, knt = _dn_prep(proj, conv_w, DN_QK_WIDTH, DN_QK_WIDTH, True, 1.0, True)
    vs = _dn_prep(proj, conv_w, 2 * DN_QK_WIDTH, DN_V_WIDTH, False, 1.0, False)
    g_cum, beta, g_tot, e_tot = _dn_gates(ba, a_log, dt_bias)
    gt_tok = jnp.broadcast_to(g_tot[:, :, None, :], (b, nchunk, DN_CHUNK, g_tot.shape[-1])).reshape(b, s, -1)
    rows = jnp.concatenate([_per_unit(g_cum), _per_unit(beta), _per_unit(gt_tok),
                            jnp.zeros((b, DN_QK_HEADS, N_UNITS, s), F32)], axis=2)
    rows = rows.reshape(b, DN_QK_HEADS, 16, npair, DN_PAIR).transpose(0, 1, 3, 2, 4)
    etot = _per_unit(e_tot).reshape(b, DN_QK_HEADS, N_UNITS * nchunk)
    etot = jnp.broadcast_to(etot[..., None], (b, DN_QK_HEADS, N_UNITS * nchunk, DN_HEAD_DIM))
    return _dn_core(qn, kn, knt, vs, rows, etot, proj, norm_w, gate_col)


def kernel(x, mem, norm_w, mem_norm_w, w_kv_mem, w_out, pool_w_in, pool_w_group, pool_scale,
           dn_w_in, dn_conv_w, dn_a_log, dn_dt_bias, dn_norm_w, final_norm_w):
    b, s, d = x.shape
    m = mem.shape[1]
    depth = norm_w.shape[0]
    x2d = x.reshape(b * s, d)
    mem2d = mem.reshape(b * m, d)
    tm = min(2048, b * s)
    tr = min(512, b * s)
    for i in range(depth):
        j = i // 2
        hm = _rmsnorm(mem2d, mem_norm_w[i], BF16, min(512, b * m))
        kv = _proj(hm, w_kv_mem, i, 0, 2 * XA_WIDTH, BF16, min(1024, b * m), 1024)
        kv = kv.reshape(b, m, 2 * XA_WIDTH)
        h = _rmsnorm(x2d, norm_w[i], BF16, tr)
        if i % 2 == 0:
            proj = _proj(h, pool_w_in, j, 0, pool_w_in.shape[2], BF16, tm, 1024)
            proj = proj.reshape(b, s, -1)
            gate_col = POOL_WIDTH + XA_WIDTH
            ym = _pool_mix(proj, pool_w_group[j].astype(BF16), pool_scale[j], gate_col)
            ya = _xattn(proj, kv, POOL_WIDTH, gate_col + POOL_WIDTH, min(512, s))
        else:
            proj = _proj(h, dn_w_in, j, 0, DN_MAIN, BF16, tm, 1024)
            proj = proj.reshape(b, s, -1)
            ba = _proj(h, dn_w_in, j, DN_MAIN, 4 * DN_V_HEADS, F32, tm, 4 * DN_V_HEADS)
            ba = ba.reshape(b, s, -1)
            gate_col = DN_CONV_CH + XA_WIDTH
            ym = _deltanet_mixer(proj, ba, dn_conv_w[j], dn_a_log[j], dn_dt_bias[j], dn_norm_w[j], gate_col)
            ya = _xattn(proj, kv, DN_CONV_CH, gate_col + DN_V_WIDTH, min(512, s))
        x2d = _out_proj(ym.reshape(b * s, -1), ya.reshape(b * s, -1), w_out, i, x2d, min(1024, b * s), 512)
    return _rmsnorm(x2d, final_norm_w, F32, tr).reshape(b, s, d)
```

```python
import functools

import jax
import jax.numpy as jnp
from jax import lax
from jax.experimental import pallas as pl
from jax.experimental.pallas import tpu as pltpu

F32 = jnp.float32
BF16 = jnp.bfloat16

EPS = 1e-6
D_MODEL = 2048
N_MEM = 256
XA_HEADS = 4
XA_HEAD_DIM = 512
XA_WIDTH = XA_HEADS * XA_HEAD_DIM
POOL_GROUPS = 4
POOL_GROUP_WIDTH = 1024
POOL_WIDTH = POOL_GROUPS * POOL_GROUP_WIDTH
DN_QK_HEADS = 16
DN_V_HEADS = 32
DN_HEAD_DIM = 128
DN_QK_WIDTH = DN_QK_HEADS * DN_HEAD_DIM
DN_V_WIDTH = DN_V_HEADS * DN_HEAD_DIM
DN_CONV_CH = 2 * DN_QK_WIDTH + DN_V_WIDTH
DN_CONV_TAPS = 5
DN_CHUNK = 64
DN_PAIR = 2 * DN_CHUNK
INNER = POOL_WIDTH + XA_WIDTH
DN_MAIN = DN_CONV_CH + XA_WIDTH + INNER

VMEM_LIMIT = 56 * 1024 * 1024


def _params(sem):
    return pltpu.CompilerParams(dimension_semantics=sem, vmem_limit_bytes=VMEM_LIMIT)


def _silu(g):
    return g * jax.nn.sigmoid(g)


def _rmsnorm_kernel(x_ref, nw_ref, o_ref):
    xf = x_ref[...]
    ms = jnp.mean(xf * xf, axis=-1, keepdims=True)
    o_ref[...] = (xf * lax.rsqrt(ms + EPS) * nw_ref[...]).astype(o_ref.dtype)


def _rmsnorm(x2d, nw, out_dtype, tm):
    m, k = x2d.shape
    return pl.pallas_call(
        _rmsnorm_kernel,
        grid=(m // tm,),
        in_specs=[pl.BlockSpec((tm, k), lambda i: (i, 0)),
                  pl.BlockSpec((1, k), lambda i: (0, 0))],
        out_specs=pl.BlockSpec((tm, k), lambda i: (i, 0)),
        out_shape=jax.ShapeDtypeStruct((m, k), out_dtype),
        compiler_params=_params(("parallel",)),
        name="rmsnorm",
    )(x2d, nw.reshape(1, k))


def _proj_kernel(h_ref, w_ref, o_ref, wb_ref):
    @pl.when(pl.program_id(1) == 0)
    def _():
        wb_ref[...] = w_ref[...].astype(BF16)

    o_ref[...] = jnp.dot(h_ref[...], wb_ref[...], preferred_element_type=F32).astype(o_ref.dtype)


def _proj(h2d, w_stack, layer, col0, ncols, out_dtype, tm, tn):
    m, k = h2d.shape
    cb = col0 // tn
    return pl.pallas_call(
        _proj_kernel,
        grid=(ncols // tn, m // tm),
        in_specs=[pl.BlockSpec((tm, k), lambda j, i: (i, 0)),
                  pl.BlockSpec((None, k, tn), lambda j, i: (layer, 0, cb + j))],
        out_specs=pl.BlockSpec((tm, tn), lambda j, i: (i, j)),
        out_shape=jax.ShapeDtypeStruct((m, ncols), out_dtype),
        scratch_shapes=[pltpu.VMEM((k, tn), BF16)],
        compiler_params=_params(("parallel", "arbitrary")),
        name="proj",
    )(h2d, w_stack)


def _out_proj_kernel(ym_ref, ya_ref, w1_ref, w2_ref, x_ref, o_ref, w1b_ref, w2b_ref):
    @pl.when(pl.program_id(1) == 0)
    def _():
        w1b_ref[...] = w1_ref[...].astype(BF16)
        w2b_ref[...] = w2_ref[...].astype(BF16)

    acc = jnp.dot(ym_ref[...], w1b_ref[...], preferred_element_type=F32)
    acc = acc + jnp.dot(ya_ref[...], w2b_ref[...], preferred_element_type=F32)
    o_ref[...] = x_ref[...] + acc


def _out_proj(ym, ya, w_stack, layer, x2d, tm, tn):
    m, n = x2d.shape
    k1, k2 = ym.shape[1], ya.shape[1]
    assert k1 % k2 == 0
    return pl.pallas_call(
        _out_proj_kernel,
        grid=(n // tn, m // tm),
        in_specs=[pl.BlockSpec((tm, k1), lambda j, i: (i, 0)),
                  pl.BlockSpec((tm, k2), lambda j, i: (i, 0)),
                  pl.BlockSpec((None, k1, tn), lambda j, i: (layer, 0, j), pipeline_mode=pl.Buffered(1)),
                  pl.BlockSpec((None, k2, tn), lambda j, i: (layer, k1 // k2, j), pipeline_mode=pl.Buffered(1)),
                  pl.BlockSpec((tm, tn), lambda j, i: (i, j))],
        out_specs=pl.BlockSpec((tm, tn), lambda j, i: (i, j)),
        out_shape=jax.ShapeDtypeStruct((m, n), F32),
        scratch_shapes=[pltpu.VMEM((k1, tn), BF16), pltpu.VMEM((k2, tn), BF16)],
        compiler_params=_params(("parallel", "arbitrary")),
        name="out_proj",
    )(ym, ya, w_stack, w_stack, x2d)


def _xattn_kernel(q_ref, k_ref, v_ref, g_ref, o_ref):
    hd = XA_HEAD_DIM
    heads = [slice(h * hd, (h + 1) * hd) for h in range(XA_HEADS)]
    scores = [lax.dot_general(q_ref[0, :, hs], k_ref[0, :, hs], (((1,), (1,)), ((), ())),
                              preferred_element_type=F32) for hs in heads]
    probs, denoms = [], []
    for s in scores:
        s = s * (hd ** -0.5)
        p = jnp.exp(s - jnp.max(s, axis=-1, keepdims=True))
        denoms.append(jnp.sum(p, axis=-1, keepdims=True))
        probs.append(p.astype(BF16))
    outs = [jnp.dot(p, v_ref[0, :, hs], preferred_element_type=F32) for p, hs in zip(probs, heads)]
    for o, l, hs in zip(outs, denoms, heads):
        o_ref[0, :, hs] = (o / l * _silu(g_ref[0, :, hs].astype(F32))).astype(o_ref.dtype)


def _xattn(proj, kv, q_col, gate_col, ts):
    b, s, _ = proj.shape
    w = XA_WIDTH
    qb, gb = q_col // w, gate_col // w
    return pl.pallas_call(
        _xattn_kernel,
        grid=(b, s // ts),
        in_specs=[pl.BlockSpec((1, ts, w), lambda bi, t: (bi, t, qb)),
                  pl.BlockSpec((1, N_MEM, w), lambda bi, t: (bi, 0, 0)),
                  pl.BlockSpec((1, N_MEM, w), lambda bi, t: (bi, 0, 1)),
                  pl.BlockSpec((1, ts, w), lambda bi, t: (bi, t, gb))],
        out_specs=pl.BlockSpec((1, ts, w), lambda bi, t: (bi, t, 0)),
        out_shape=jax.ShapeDtypeStruct((b, s, XA_WIDTH), BF16),
        compiler_params=_params(("parallel", "parallel")),
        name="mem_xattn",
    )(proj, kv, kv, proj)


POOL_TILE = 128
POOL_SUBTILES = 2
POOL_HALO = 64


def _pool_kernel(u_ref, g_ref, wg_ref, sc_ref, o_ref, ext_ref):
    s = u_ref.shape[1]
    gw = u_ref.shape[2]
    grp = pl.program_id(1)
    half = jnp.left_shift(1, grp)
    ext_ref[0:POOL_HALO, :] = jnp.zeros((POOL_HALO, gw), BF16)
    ext_ref[POOL_HALO + s:POOL_HALO + s + POOL_HALO, :] = jnp.zeros((POOL_HALO, gw), BF16)
    ext_ref[POOL_HALO:POOL_HALO + s, :] = u_ref[0]

    kk = POOL_TILE + 2 * POOL_HALO
    r = lax.broadcasted_iota(jnp.int32, (POOL_TILE, kk), 0)
    c = lax.broadcasted_iota(jnp.int32, (POOL_TILE, kk), 1)
    d = c - POOL_HALO - r
    band = jnp.where((d >= -half) & (d < half), 1.0, 0.0).astype(BF16)
    wg = wg_ref[0]
    scale = sc_ref[0]

    rows = POOL_SUBTILES * POOL_TILE

    def body(i, carry):
        base = pl.multiple_of(i * rows, rows)
        pooled = []
        for sub in range(POOL_SUBTILES):
            t0 = base + sub * POOL_TILE
            wsum = jnp.dot(band, ext_ref[pl.ds(t0, kk), :], preferred_element_type=F32)
            tok = t0 + lax.broadcasted_iota(jnp.int32, (POOL_TILE, gw), 0)
            cnt = (jnp.minimum(tok + half, s) - jnp.maximum(tok - half, 0)).astype(F32)
            centre = ext_ref[pl.ds(t0 + POOL_HALO, POOL_TILE), :].astype(F32)
            pooled.append((wsum / cnt - centre).astype(BF16))
        mixed = jnp.dot(jnp.concatenate(pooled, axis=0), wg, preferred_element_type=F32) * scale
        gate = g_ref[0, pl.ds(base, rows), :].astype(F32)
        o_ref[0, pl.ds(base, rows), :] = (mixed * _silu(gate)).astype(o_ref.dtype)
        return carry

    lax.fori_loop(0, s // rows, body, 0)


def _pool_mix(proj, wg, scale, gate_col):
    b, s, _ = proj.shape
    gw = POOL_GROUP_WIDTH
    gb = gate_col // gw
    return pl.pallas_call(
        _pool_kernel,
        grid=(b, POOL_GROUPS),
        in_specs=[pl.BlockSpec((1, s, gw), lambda bi, g: (bi, 0, g)),
                  pl.BlockSpec((1, s, gw), lambda bi, g: (bi, 0, gb + g)),
                  pl.BlockSpec((1, gw, gw), lambda bi, g: (g, 0, 0)),
                  pl.BlockSpec((1, 1, gw), lambda bi, g: (g, 0, 0))],
        out_specs=pl.BlockSpec((1, s, gw), lambda bi, g: (bi, 0, g)),
        out_shape=jax.ShapeDtypeStruct((b, s, POOL_WIDTH), BF16),
        scratch_shapes=[pltpu.VMEM((s + 2 * POOL_HALO, gw), BF16)],
        compiler_params=_params(("parallel", "parallel")),
        name="pool_mix",
    )(proj, proj, wg, scale.reshape(POOL_GROUPS, 1, gw))


CONV_PAD = 8
CONV_HEADS_PER_STEP = 4


def _dn_prep_kernel(x_ref, cw_ref, *refs, normalize, scale, emit_t):
    if emit_t:
        o_ref, ot_ref, ext_ref = refs
    else:
        o_ref, ext_ref = refs
    s = x_ref.shape[1]
    width = x_ref.shape[2]
    hd = DN_HEAD_DIM
    for hl in range(width // hd):
        ext_ref[hl, 0:CONV_PAD, :] = jnp.zeros((CONV_PAD, hd), F32)
        ext_ref[hl, CONV_PAD + s:2 * CONV_PAD + s, :] = jnp.zeros((CONV_PAD, hd), F32)
        ext_ref[hl, CONV_PAD:CONV_PAD + s, :] = x_ref[0, :, hl * hd:(hl + 1) * hd].astype(F32)
    cw = cw_ref[...]
    tile = min(s, 256)

    def body(i, carry):
        t0 = pl.multiple_of(i * tile, tile)
        for hl in range(width // hd):
            lanes = slice(hl * hd, (hl + 1) * hd)
            acc = None
            for tap in range(DN_CONV_TAPS):
                off = CONV_PAD + tap - DN_CONV_TAPS // 2
                term = ext_ref[hl, pl.ds(t0 + off, tile), :] * cw[tap:tap + 1, lanes]
                acc = term if acc is None else acc + term
            y = _silu(acc)
            if normalize:
                y = y * lax.rsqrt(jnp.sum(y * y, axis=-1, keepdims=True) + EPS)
                if scale != 1.0:
                    y = y * scale
            o_ref[0, pl.ds(t0, tile), lanes] = y.astype(o_ref.dtype)
            if emit_t:
                for h in range(tile // DN_PAIR):
                    ot_ref[0, hl, i * (tile // DN_PAIR) + h] = (
                        y[h * DN_PAIR:(h + 1) * DN_PAIR, :].T.astype(ot_ref.dtype))
        return carry

    lax.fori_loop(0, s // tile, body, 0)


def _dn_prep(proj, conv_w, col0, ncols, normalize, scale, emit_t):
    b, s, _ = proj.shape
    hd = DN_HEAD_DIM
    width = CONV_HEADS_PER_STEP * hd
    nh = ncols // hd
    cb = col0 // width
    out_shape = [jax.ShapeDtypeStruct((b, s, ncols), BF16)]
    out_specs = [pl.BlockSpec((1, s, width), lambda bi, h: (bi, 0, h))]
    if emit_t:
        out_shape.append(jax.ShapeDtypeStruct((b, nh, s // DN_PAIR, hd, DN_PAIR), BF16))
        out_specs.append(pl.BlockSpec((1, CONV_HEADS_PER_STEP, s // DN_PAIR, hd, DN_PAIR),
                                      lambda bi, h: (bi, h, 0, 0, 0)))
    res = pl.pallas_call(
        functools.partial(_dn_prep_kernel, normalize=normalize, scale=scale, emit_t=emit_t),
        grid=(b, ncols // width),
        in_specs=[pl.BlockSpec((1, s, width), lambda bi, h: (bi, 0, cb + h)),
                  pl.BlockSpec((DN_CONV_TAPS, width), lambda bi, h: (0, cb + h))],
        out_specs=out_specs,
        out_shape=out_shape,
        scratch_shapes=[pltpu.VMEM((CONV_HEADS_PER_STEP, s + 2 * CONV_PAD, hd), F32)],
        compiler_params=_params(("parallel", "parallel")),
        name="dn_conv",
    )(proj, conv_w)
    return res if emit_t else res[0]


def _dn_gates_kernel(ba_ref, alog_ref, dtb_ref, g_ref, beta_ref, gt_ref, et_ref):
    s = ba_ref.shape[1]
    nh2 = 2 * DN_V_HEADS
    ba = ba_ref[0]
    beta_ref[0] = jax.nn.sigmoid(ba[:, :nh2])
    z = ba[:, nh2:] + dtb_ref[...]
    softplus = jnp.maximum(z, 0.0) + jnp.log(1.0 + jnp.exp(-jnp.abs(z)))
    g = -jnp.exp(alog_ref[...]) * softplus
    r = lax.broadcasted_iota(jnp.int32, (DN_CHUNK, DN_CHUNK), 0)
    c = lax.broadcasted_iota(jnp.int32, (DN_CHUNK, DN_CHUNK), 1)
    lower = jnp.where(r >= c, 1.0, 0.0).astype(F32)
    upper = jnp.where(r <= c, 1.0, 0.0).astype(F32)
    is_fwd = lax.broadcasted_iota(jnp.int32, (DN_CHUNK, nh2), 1) < DN_V_HEADS
    for n in range(s // DN_CHUNK):
        gc = g[n * DN_CHUNK:(n + 1) * DN_CHUNK, :]
        pre = jnp.dot(lower, gc, preferred_element_type=F32, precision=lax.Precision.HIGHEST)
        suf = jnp.dot(upper, gc, preferred_element_type=F32, precision=lax.Precision.HIGHEST)
        g_ref[0, n * DN_CHUNK:(n + 1) * DN_CHUNK, :] = jnp.where(is_fwd, pre, suf)
        g_tot = jnp.sum(gc, axis=0, keepdims=True)
        gt_ref[0, n:n + 1, :] = g_tot
        et_ref[0, n:n + 1, :] = jnp.exp(g_tot)


def _dn_gates(ba, a_log, dt_bias):
    b, s, w = ba.shape
    nh2 = 2 * DN_V_HEADS
    nchunk = s // DN_CHUNK
    return pl.pallas_call(
        _dn_gates_kernel,
        grid=(b,),
        in_specs=[pl.BlockSpec((1, s, w), lambda bi: (bi, 0, 0)),
                  pl.BlockSpec((1, nh2), lambda bi: (0, 0)),
                  pl.BlockSpec((1, nh2), lambda bi: (0, 0))],
        out_specs=[pl.BlockSpec((1, s, nh2), lambda bi: (bi, 0, 0)),
                   pl.BlockSpec((1, s, nh2), lambda bi: (bi, 0, 0)),
                   pl.BlockSpec((1, nchunk, nh2), lambda bi: (bi, 0, 0)),
                   pl.BlockSpec((1, nchunk, nh2), lambda bi: (bi, 0, 0))],
        out_shape=[jax.ShapeDtypeStruct((b, s, nh2), F32),
                   jax.ShapeDtypeStruct((b, s, nh2), F32),
                   jax.ShapeDtypeStruct((b, nchunk, nh2), F32),
                   jax.ShapeDtypeStruct((b, nchunk, nh2), F32)],
        compiler_params=_params(("parallel",)),
        name="dn_gates",
    )(ba, a_log.reshape(1, nh2), dt_bias.reshape(1, nh2))


N_UNITS = 4
DN_HEADS_PER_STEP = 2
ROW_G, ROW_BETA, ROW_GT = 0, 4, 8


def _dn_core_kernel(q_ref, k_ref, kt_ref, v_ref, r_ref, et_ref, gate_ref, nw_ref, o_ref,
                    u_sc, w_sc, a_sc, qd_sc, kd_sc, st_ref, of_ref, ob_ref):
    s = q_ref.shape[1]
    hd = DN_HEAD_DIM
    pr = DN_PAIR
    npair = s // pr
    nchunk = 2 * npair

    row = lax.broadcasted_iota(jnp.int32, (pr, pr), 0)
    col = lax.broadcasted_iota(jnp.int32, (pr, pr), 1)
    same = jnp.right_shift(row, 6) == jnp.right_shift(col, 6)
    masks = {False: (same & (row >= col), same & (row > col)),
             True: (same & (row <= col), same & (row < col))}
    zeros_half = jnp.zeros((DN_CHUNK, hd), F32)
    zeros_blk = jnp.zeros((DN_CHUNK, pr), BF16)
    zeros_rhs = jnp.zeros((DN_CHUNK, 2 * hd), BF16)
    frow = lax.broadcasted_iota(jnp.int32, (DN_CHUNK, pr), 0)
    fcol = lax.broadcasted_iota(jnp.int32, (DN_CHUNK, pr), 1)
    left = fcol < DN_CHUNK
    eye_folded = jnp.where((fcol == frow) | (fcol == frow + DN_CHUNK), 1.0, 0.0).astype(F32)

    def mm(a, b):
        return jnp.dot(a, b, preferred_element_type=F32)

    nheads = DN_HEADS_PER_STEP
    half = npair // 2
    nw = nw_ref[...]

    def solve_pairs(pair_ids):
        heads = []
        for pi in pair_ids:
            t0 = pl.multiple_of(pi * pr, pr)
            for hl in range(nheads):
                heads.append((hl, pi, q_ref[0, pl.ds(t0, pr), hl * hd:(hl + 1) * hd],
                              k_ref[0, pl.ds(t0, pr), hl * hd:(hl + 1) * hd], kt_ref[0, hl, pi],
                              v_ref[0, pl.ds(t0, pr), 2 * hl * hd:2 * (hl + 1) * hd], r_ref[0, hl, pi]))
        yield
        both = [mm(jnp.concatenate([q2, k2], axis=0), kt2) for _, _, q2, k2, kt2, _, _ in heads]
        yield
        chains = []
        for (hl, pi, q2, k2, kt2, v2, rows), qkk in zip(heads, both):
            qk, kk = qkk[:pr], qkk[pr:]
            q2f, k2f, kt2f = q2.astype(F32), k2.astype(F32), kt2.astype(F32)
            for sub in range(N_UNITS):
                unit = hl * N_UNITS + sub
                incl, strict = masks[sub >= 2]
                g_row = rows[ROW_G + sub:ROW_G + sub + 1]
                b_row = rows[ROW_BETA + sub:ROW_BETA + sub + 1]
                gt_row = rows[ROW_GT + sub:ROW_GT + sub + 1]
                m2 = jnp.broadcast_to(g_row, (pr, pr))
                m1 = m2.T
                dm = jnp.where(incl, jnp.exp(jnp.where(incl, m1 - m2, 0.0)), 0.0)
                kt_mat = jnp.where(strict, kk * dm, 0.0) * b_row
                e1 = jnp.exp(m1)
                vh = sub % 2
                a_sc[unit, pi] = (qk * dm * b_row).astype(BF16)
                qd_sc[unit, pi] = (q2f * e1).astype(BF16)
                kd_sc[unit, pi] = (kt2f * (b_row * jnp.exp(gt_row - g_row))).astype(BF16)
                rhs = jnp.concatenate([v2[:, vh * hd:(vh + 1) * hd], (k2f * e1).astype(BF16)], axis=1)
                chains.append(dict(
                    unit=unit, pi=pi, p=jnp.where(left, -kt_mat[:DN_CHUNK], -kt_mat[DN_CHUNK:]), x=eye_folded,
                    rhs=jnp.concatenate(
                        [jnp.concatenate([rhs[DN_CHUNK:], zeros_rhs], axis=1),
                         jnp.concatenate([zeros_rhs, rhs[:DN_CHUNK]], axis=1)], axis=0)))
        for stage in range(6):
            outs = []
            for ch in chains:
                pb, xb = ch["p"].astype(BF16), ch["x"].astype(BF16)
                blk_a = jnp.where(left, pb, xb)
                blk_b = jnp.where(left, xb, pb)
                outs.append(mm(pb, jnp.concatenate(
                    [jnp.concatenate([blk_a, zeros_blk], axis=1),
                     jnp.concatenate([zeros_blk, blk_b], axis=1)], axis=0)))
            yield
            for ch, out in zip(chains, outs):
                ch["x"] = ch["x"] + jnp.where(left, out[:, pr:], out[:, :pr])
                if stage < 5:
                    ch["p"] = jnp.where(left, out[:, :pr], out[:, pr:])
        sols = [mm(ch["x"].astype(BF16), ch["rhs"]) for ch in chains]
        yield
        for ch, sol in zip(chains, sols):
            unit, pi = ch["unit"], ch["pi"]
            u_sc[unit, pi] = jnp.concatenate([sol[:, 2 * hd:3 * hd], sol[:, :hd]], axis=0)
            w_sc[unit, pi] = jnp.concatenate([sol[:, 3 * hd:], sol[:, hd:2 * hd]], axis=0).astype(BF16)

    def recur_step(it, finalize):
        units = []
        for unit in range(nheads * N_UNITS):
            hl, sub = divmod(unit, N_UNITS)
            backward = sub >= 2
            pi = (npair - 1 - it) if backward else it
            units.append(dict(
                unit=unit, backward=backward, pi=pi, vhead=2 * hl + sub % 2,
                u=u_sc[unit, pi], w=w_sc[unit, pi], a=a_sc[unit, pi],
                qd=qd_sc[unit, pi], kd=kd_sc[unit, pi], st=st_ref[unit], outs=[None, None],
                decay=[et_ref[0, hl, pl.ds(sub * nchunk + 2 * pi + c, 1), :] for c in range(2)]))
        if finalize:
            for un in units:
                t0 = pl.multiple_of(un["pi"] * pr, pr)
                lanes = slice(un["vhead"] * hd, (un["vhead"] + 1) * hd)
                if un["backward"]:
                    un["other"] = of_ref[un["vhead"], pl.ds(t0, pr), :]
                else:
                    un["other"] = ob_ref[un["vhead"], pl.ds(pl.multiple_of((un["pi"] - half) * pr, pr), pr), :]
                un["gate"] = gate_ref[0, pl.ds(t0, pr), lanes]
        yield
        for step in range(2):
            for un in units:
                c = (1 - step) if un["backward"] else step
                un["c"] = c
                un["rows"] = slice(c * DN_CHUNK, (c + 1) * DN_CHUNK)
            rss = [mm(jnp.concatenate([un["w"][un["rows"]], un["qd"][un["rows"]]], axis=0),
                      un["st"].astype(BF16)) for un in units]
            yield
            for un, rs in zip(units, rss):
                halves = [zeros_half, zeros_half]
                halves[un["c"]] = un["u"][un["rows"]] - rs[:DN_CHUNK]
                un["v_full"] = jnp.concatenate(halves, axis=0).astype(BF16)
                un["qs"] = rs[DN_CHUNK:]
            intra = [mm(un["a"][un["rows"]], un["v_full"]) for un in units]
            upd = [mm(un["kd"], un["v_full"]) for un in units]
            yield
            for un, o_in, st_up in zip(units, intra, upd):
                un["outs"][un["c"]] = un["qs"] + o_in
                un["st"] = un["st"] * un["decay"][un["c"]] + st_up
        for un in units:
            t0 = pl.multiple_of(un["pi"] * pr, pr)
            o_pair = jnp.concatenate(un["outs"], axis=0)
            if finalize:
                o = o_pair + un["other"]
                o = o * lax.rsqrt(jnp.mean(o * o, axis=-1, keepdims=True) + EPS) * nw
                lanes = slice(un["vhead"] * hd, (un["vhead"] + 1) * hd)
                o_ref[0, pl.ds(t0, pr), lanes] = (o * _silu(un["gate"].astype(F32))).astype(o_ref.dtype)
            elif un["backward"]:
                ob_ref[un["vhead"], pl.ds(pl.multiple_of((un["pi"] - half) * pr, pr), pr), :] = o_pair
            else:
                of_ref[un["vhead"], pl.ds(t0, pr), :] = o_pair
            st_ref[un["unit"]] = un["st"]

    def run(*gens_and_order):
        gens, order = gens_and_order[:-1], gens_and_order[-1]
        for g in order:
            next(gens[g])
        for g in gens:
            for _ in g:
                pass

    MERGED_ORDER = (0, 1, 0, 1, 0, 0, 1, 0, 0, 1, 0, 0, 1, 0)

    st_ref[...] = jnp.zeros(st_ref.shape, F32)
    run(solve_pairs([0, npair - 1]), ())

    def merged(it, carry):
        run(solve_pairs([it + 1, npair - 2 - it]), recur_step(it, False), MERGED_ORDER)
        return carry

    lax.fori_loop(0, half - 1, merged, 0)
    run(recur_step(half - 1, False), ())

    def second_half(it, carry):
        run(recur_step(it, True), ())
        return carry

    lax.fori_loop(half, npair, second_half, 0)


def _dn_core(qn, kn, knt, vs, rows, etot, proj, norm_w, gate_col):
    b, s, _ = qn.shape
    hd = DN_HEAD_DIM
    npair = s // DN_PAIR
    nh = DN_HEADS_PER_STEP
    nu = nh * N_UNITS
    gb = gate_col // (2 * nh * hd)
    return pl.pallas_call(
        _dn_core_kernel,
        grid=(b, DN_QK_HEADS // nh),
        in_specs=[pl.BlockSpec((1, s, nh * hd), lambda bi, j: (bi, 0, j)),
                  pl.BlockSpec((1, s, nh * hd), lambda bi, j: (bi, 0, j)),
                  pl.BlockSpec((1, nh, npair, hd, DN_PAIR), lambda bi, j: (bi, j, 0, 0, 0)),
                  pl.BlockSpec((1, s, 2 * nh * hd), lambda bi, j: (bi, 0, j)),
                  pl.BlockSpec((1, nh, npair, 16, DN_PAIR), lambda bi, j: (bi, j, 0, 0, 0)),
                  pl.BlockSpec((1, nh, N_UNITS * 2 * npair, hd), lambda bi, j: (bi, j, 0, 0)),
                  pl.BlockSpec((1, s, 2 * nh * hd), lambda bi, j: (bi, 0, gb + j)),
                  pl.BlockSpec((1, hd), lambda bi, j: (0, 0))],
        out_specs=pl.BlockSpec((1, s, 2 * nh * hd), lambda bi, j: (bi, 0, j)),
        out_shape=jax.ShapeDtypeStruct((b, s, DN_V_WIDTH), BF16),
        scratch_shapes=[pltpu.VMEM((nu, npair, DN_PAIR, hd), F32),
                        pltpu.VMEM((nu, npair, DN_PAIR, hd), BF16),
                        pltpu.VMEM((nu, npair, DN_PAIR, DN_PAIR), BF16),
                        pltpu.VMEM((nu, npair, DN_PAIR, hd), BF16),
                        pltpu.VMEM((nu, npair, hd, DN_PAIR), BF16),
                        pltpu.VMEM((nu, hd, hd), F32),
                        pltpu.VMEM((2 * nh, s // 2, hd), F32),
                        pltpu.VMEM((2 * nh, s // 2, hd), F32)],
        compiler_params=_params(("parallel", "parallel")),
        name="dn_core",
    )(qn, kn, knt, vs, rows, etot, proj, norm_w.reshape(1, hd))


def _per_unit(a):
    b, t, _ = a.shape
    a = a.reshape(b, t, 2, DN_QK_HEADS, 2)
    return a.transpose(0, 3, 2, 4, 1).reshape(b, DN_QK_HEADS, N_UNITS, t)


def _deltanet_mixer(proj, ba, conv_w, a_log, dt_bias, norm_w, gate_col):
    b, s, _ = proj.shape
    npair = s // DN_PAIR
    nchunk = s // DN_CHUNK
    qn = _dn_prep(proj, conv_w, 0, DN_QK_WIDTH, True, DN_HEAD_DIM ** -0.5, False)
    kn, knt = _dn_prep(proj, conv_w, DN_QK_WIDTH, DN_QK_WIDTH, True, 1.0, True)
    vs = _dn_prep(proj, conv_w, 2 * DN_QK_WIDTH, DN_V_WIDTH, False, 1.0, False)
    g_cum, beta, g_tot, e_tot = _dn_gates(ba, a_log, dt_bias)
    gt_tok = jnp.broadcast_to(g_tot[:, :, None, :], (b, nchunk, DN_CHUNK, g_tot.shape[-1])).reshape(b, s, -1)
    rows = jnp.concatenate([_per_unit(g_cum), _per_unit(beta), _per_unit(gt_tok),
                            jnp.zeros((b, DN_QK_HEADS, N_UNITS, s), F32)], axis=2)
    rows = rows.reshape(b, DN_QK_HEADS, 16, npair, DN_PAIR).transpose(0, 1, 3, 2, 4)
    etot = _per_unit(e_tot).reshape(b, DN_QK_HEADS, N_UNITS * nchunk)
    etot = jnp.broadcast_to(etot[..., None], (b, DN_QK_HEADS, N_UNITS * nchunk, DN_HEAD_DIM))
    return _dn_core(qn, kn, knt, vs, rows, etot, proj, norm_w, gate_col)


def kernel(x, mem, norm_w, mem_norm_w, w_kv_mem, w_out, pool_w_in, pool_w_group, pool_scale,
           dn_w_in, dn_conv_w, dn_a_log, dn_dt_bias, dn_norm_w, final_norm_w):
    b, s, d = x.shape
    m = mem.shape[1]
    depth = norm_w.shape[0]
    x2d = x.reshape(b * s, d)
    mem2d = mem.reshape(b * m, d)
    tm = min(2048, b * s)
    tr = min(512, b * s)
    for i in range(depth):
        j = i // 2
        hm = _rmsnorm(mem2d, mem_norm_w[i], BF16, min(512, b * m))
        kv = _proj(hm, w_kv_mem, i, 0, 2 * XA_WIDTH, BF16, min(1024, b * m), 1024)
        kv = kv.reshape(b, m, 2 * XA_WIDTH)
        h = _rmsnorm(x2d, norm_w[i], BF16, tr)
        if i % 2 == 0:
            proj = _proj(h, pool_w_in, j, 0, pool_w_in.shape[2], BF16, tm, 1024)
            proj = proj.reshape(b, s, -1)
            gate_col = POOL_WIDTH + XA_WIDTH
            ym = _pool_mix(proj, pool_w_group[j].astype(BF16), pool_scale[j], gate_col)
            ya = _xattn(proj, kv, POOL_WIDTH, gate_col + POOL_WIDTH, min(512, s))
        else:
            proj = _proj(h, dn_w_in, j, 0, DN_MAIN, BF16, tm, 1024)
            proj = proj.reshape(b, s, -1)
            ba = _proj(h, dn_w_in, j, DN_MAIN, 4 * DN_V_HEADS, F32, tm, 4 * DN_V_HEADS)
            ba = ba.reshape(b, s, -1)
            gate_col = DN_CONV_CH + XA_WIDTH
            ym = _deltanet_mixer(proj, ba, dn_conv_w[j], dn_a_log[j], dn_dt_bias[j], dn_norm_w[j], gate_col)
            ya = _xattn(proj, kv, DN_CONV_CH, gate_col + DN_V_WIDTH, min(512, s))
        x2d = _out_proj(ym.reshape(b * s, -1), ya.reshape(b * s, -1), w_out, i, x2d, min(1024, b * s), 512)
    return _rmsnorm(x2d, final_norm_w, F32, tr).reshape(b, s, d)
```

```python
import functools

import jax
import jax.numpy as jnp
from jax import lax
from jax.experimental import pallas as pl
from jax.experimental.pallas import tpu as pltpu

F32 = jnp.float32
BF16 = jnp.bfloat16

EPS = 1e-6
D_MODEL = 2048
N_MEM = 256
XA_HEADS = 4
XA_HEAD_DIM = 512
XA_WIDTH = XA_HEADS * XA_HEAD_DIM
POOL_GROUPS = 4
POOL_GROUP_WIDTH = 1024
POOL_WIDTH = POOL_GROUPS * POOL_GROUP_WIDTH
DN_QK_HEADS = 16
DN_V_HEADS = 32
DN_HEAD_DIM = 128
DN_QK_WIDTH = DN_QK_HEADS * DN_HEAD_DIM
DN_V_WIDTH = DN_V_HEADS * DN_HEAD_DIM
DN_CONV_CH = 2 * DN_QK_WIDTH + DN_V_WIDTH
DN_CONV_TAPS = 5
DN_CHUNK = 64
DN_PAIR = 2 * DN_CHUNK
INNER = POOL_WIDTH + XA_WIDTH
DN_MAIN = DN_CONV_CH + XA_WIDTH + INNER

VMEM_LIMIT = 56 * 1024 * 1024


def _params(sem):
    return pltpu.CompilerParams(dimension_semantics=sem, vmem_limit_bytes=VMEM_LIMIT)


def _silu(g):
    return g * jax.nn.sigmoid(g)


def _rmsnorm_kernel(x_ref, nw_ref, o_ref):
    xf = x_ref[...]
    ms = jnp.mean(xf * xf, axis=-1, keepdims=True)
    o_ref[...] = (xf * lax.rsqrt(ms + EPS) * nw_ref[...]).astype(o_ref.dtype)


def _rmsnorm(x2d, nw, out_dtype, tm):
    m, k = x2d.shape
    return pl.pallas_call(
        _rmsnorm_kernel,
        grid=(m // tm,),
        in_specs=[pl.BlockSpec((tm, k), lambda i: (i, 0)),
                  pl.BlockSpec((1, k), lambda i: (0, 0))],
        out_specs=pl.BlockSpec((tm, k), lambda i: (i, 0)),
        out_shape=jax.ShapeDtypeStruct((m, k), out_dtype),
        compiler_params=_params(("parallel",)),
        name="rmsnorm",
    )(x2d, nw.reshape(1, k))


def _proj_kernel(h_ref, w_ref, o_ref, wb_ref):
    @pl.when(pl.program_id(1) == 0)
    def _():
        wb_ref[...] = w_ref[...].astype(BF16)

    o_ref[...] = jnp.dot(h_ref[...], wb_ref[...], preferred_element_type=F32).astype(o_ref.dtype)


def _proj(h2d, w_stack, layer, col0, ncols, out_dtype, tm, tn):
    m, k = h2d.shape
    cb = col0 // tn
    return pl.pallas_call(
        _proj_kernel,
        grid=(ncols // tn, m // tm),
        in_specs=[pl.BlockSpec((tm, k), lambda j, i: (i, 0)),
                  pl.BlockSpec((None, k, tn), lambda j, i: (layer, 0, cb + j))],
        out_specs=pl.BlockSpec((tm, tn), lambda j, i: (i, j)),
        out_shape=jax.ShapeDtypeStruct((m, ncols), out_dtype),
        scratch_shapes=[pltpu.VMEM((k, tn), BF16)],
        compiler_params=_params(("parallel", "arbitrary")),
        name="proj",
    )(h2d, w_stack)


def _out_proj_kernel(ym_ref, ya_ref, w1_ref, w2_ref, x_ref, o_ref, w1b_ref, w2b_ref):
    @pl.when(pl.program_id(1) == 0)
    def _():
        w1b_ref[...] = w1_ref[...].astype(BF16)
        w2b_ref[...] = w2_ref[...].astype(BF16)

    acc = jnp.dot(ym_ref[...], w1b_ref[...], preferred_element_type=F32)
    acc = acc + jnp.dot(ya_ref[...], w2b_ref[...], preferred_element_type=F32)
    o_ref[...] = x_ref[...] + acc


def _out_proj(ym, ya, w_stack, layer, x2d, tm, tn):
    m, n = x2d.shape
    k1, k2 = ym.shape[1], ya.shape[1]
    assert k1 % k2 == 0
    return pl.pallas_call(
        _out_proj_kernel,
        grid=(n // tn, m // tm),
        in_specs=[pl.BlockSpec((tm, k1), lambda j, i: (i, 0)),
                  pl.BlockSpec((tm, k2), lambda j, i: (i, 0)),
                  pl.BlockSpec((None, k1, tn), lambda j, i: (layer, 0, j), pipeline_mode=pl.Buffered(1)),
                  pl.BlockSpec((None, k2, tn), lambda j, i: (layer, k1 // k2, j), pipeline_mode=pl.Buffered(1)),
                  pl.BlockSpec((tm, tn), lambda j, i: (i, j))],
        out_specs=pl.BlockSpec((tm, tn), lambda j, i: (i, j)),
        out_shape=jax.ShapeDtypeStruct((m, n), F32),
        scratch_shapes=[pltpu.VMEM((k1, tn), BF16), pltpu.VMEM((k2, tn), BF16)],
        compiler_params=_params(("parallel", "arbitrary")),
        name="out_proj",
    )(ym, ya, w_stack, w_stack, x2d)


def _xattn_kernel(q_ref, k_ref, v_ref, g_ref, o_ref):
    hd = XA_HEAD_DIM
    heads = [slice(h * hd, (h + 1) * hd) for h in range(XA_HEADS)]
    scores = [lax.dot_general(q_ref[0, :, hs], k_ref[0, :, hs], (((1,), (1,)), ((), ())),
                              preferred_element_type=F32) for hs in heads]
    probs, denoms = [], []
    for s in scores:
        s = s * (hd ** -0.5)
        p = jnp.exp(s - jnp.max(s, axis=-1, keepdims=True))
        denoms.append(jnp.sum(p, axis=-1, keepdims=True))
        probs.append(p.astype(BF16))
    outs = [jnp.dot(p, v_ref[0, :, hs], preferred_element_type=F32) for p, hs in zip(probs, heads)]
    for o, l, hs in zip(outs, denoms, heads):
        o_ref[0, :, hs] = (o / l * _silu(g_ref[0, :, hs].astype(F32))).astype(o_ref.dtype)


def _xattn(proj, kv, q_col, gate_col, ts):
    b, s, _ = proj.shape
    w = XA_WIDTH
    qb, gb = q_col // w, gate_col // w
    return pl.pallas_call(
        _xattn_kernel,
        grid=(b, s // ts),
        in_specs=[pl.BlockSpec((1, ts, w), lambda bi, t: (bi, t, qb)),
                  pl.BlockSpec((1, N_MEM, w), lambda bi, t: (bi, 0, 0)),
                  pl.BlockSpec((1, N_MEM, w), lambda bi, t: (bi, 0, 1)),
                  pl.BlockSpec((1, ts, w), lambda bi, t: (bi, t, gb))],
        out_specs=pl.BlockSpec((1, ts, w), lambda bi, t: (bi, t, 0)),
        out_shape=jax.ShapeDtypeStruct((b, s, XA_WIDTH), BF16),
        compiler_params=_params(("parallel", "parallel")),
        name="mem_xattn",
    )(proj, kv, kv, proj)


POOL_TILE = 128
POOL_SUBTILES = 4
POOL_HALO = 64


def _pool_kernel(u_ref, g_ref, wg_ref, sc_ref, o_ref, ext_ref):
    s = u_ref.shape[1]
    gw = u_ref.shape[2]
    grp = pl.program_id(1)
    half = jnp.left_shift(1, grp)
    ext_ref[0:POOL_HALO, :] = jnp.zeros((POOL_HALO, gw), BF16)
    ext_ref[POOL_HALO + s:POOL_HALO + s + POOL_HALO, :] = jnp.zeros((POOL_HALO, gw), BF16)
    ext_ref[POOL_HALO:POOL_HALO + s, :] = u_ref[0]

    kk = POOL_TILE + 2 * POOL_HALO
    r = lax.broadcasted_iota(jnp.int32, (POOL_TILE, kk), 0)
    c = lax.broadcasted_iota(jnp.int32, (POOL_TILE, kk), 1)
    d = c - POOL_HALO - r
    band = jnp.where((d >= -half) & (d < half), 1.0, 0.0).astype(BF16)
    wg = wg_ref[0]
    scale = sc_ref[0]

    rows = POOL_SUBTILES * POOL_TILE

    def body(i, carry):
        base = pl.multiple_of(i * rows, rows)
        pooled = []
        for sub in range(POOL_SUBTILES):
            t0 = base + sub * POOL_TILE
            wsum = jnp.dot(band, ext_ref[pl.ds(t0, kk), :], preferred_element_type=F32)
            tok = t0 + lax.broadcasted_iota(jnp.int32, (POOL_TILE, gw), 0)
            cnt = (jnp.minimum(tok + half, s) - jnp.maximum(tok - half, 0)).astype(F32)
            centre = ext_ref[pl.ds(t0 + POOL_HALO, POOL_TILE), :].astype(F32)
            pooled.append((wsum / cnt - centre).astype(BF16))
        group = 2
        mixed = [jnp.dot(jnp.concatenate(pooled[k:k + group], axis=0), wg, preferred_element_type=F32)
                 for k in range(0, POOL_SUBTILES, group)]
        for k, mx in enumerate(mixed):
            r0 = base + k * group * POOL_TILE
            gate = g_ref[0, pl.ds(r0, group * POOL_TILE), :].astype(F32)
            o_ref[0, pl.ds(r0, group * POOL_TILE), :] = (mx * scale * _silu(gate)).astype(o_ref.dtype)
        return carry

    lax.fori_loop(0, s // rows, body, 0)


def _pool_mix(proj, wg, scale, gate_col):
    b, s, _ = proj.shape
    gw = POOL_GROUP_WIDTH
    gb = gate_col // gw
    return pl.pallas_call(
        _pool_kernel,
        grid=(b, POOL_GROUPS),
        in_specs=[pl.BlockSpec((1, s, gw), lambda bi, g: (bi, 0, g)),
                  pl.BlockSpec((1, s, gw), lambda bi, g: (bi, 0, gb + g)),
                  pl.BlockSpec((1, gw, gw), lambda bi, g: (g, 0, 0)),
                  pl.BlockSpec((1, 1, gw), lambda bi, g: (g, 0, 0))],
        out_specs=pl.BlockSpec((1, s, gw), lambda bi, g: (bi, 0, g)),
        out_shape=jax.ShapeDtypeStruct((b, s, POOL_WIDTH), BF16),
        scratch_shapes=[pltpu.VMEM((s + 2 * POOL_HALO, gw), BF16)],
        compiler_params=_params(("parallel", "parallel")),
        name="pool_mix",
    )(proj, proj, wg, scale.reshape(POOL_GROUPS, 1, gw))


CONV_PAD = 8
CONV_HEADS_PER_STEP = 4


def _dn_prep_kernel(x_ref, cw_ref, *refs, normalize, scale, emit_t):
    if emit_t:
        o_ref, ot_ref, ext_ref = refs
    else:
        o_ref, ext_ref = refs
    s = x_ref.shape[1]
    width = x_ref.shape[2]
    hd = DN_HEAD_DIM
    for hl in range(width // hd):
        ext_ref[hl, 0:CONV_PAD, :] = jnp.zeros((CONV_PAD, hd), F32)
        ext_ref[hl, CONV_PAD + s:2 * CONV_PAD + s, :] = jnp.zeros((CONV_PAD, hd), F32)
        ext_ref[hl, CONV_PAD:CONV_PAD + s, :] = x_ref[0, :, hl * hd:(hl + 1) * hd].astype(F32)
    cw = cw_ref[...]
    tile = min(s, 256)

    def body(i, carry):
        t0 = pl.multiple_of(i * tile, tile)
        for hl in range(width // hd):
            lanes = slice(hl * hd, (hl + 1) * hd)
            acc = None
            for tap in range(DN_CONV_TAPS):
                off = CONV_PAD + tap - DN_CONV_TAPS // 2
                term = ext_ref[hl, pl.ds(t0 + off, tile), :] * cw[tap:tap + 1, lanes]
                acc = term if acc is None else acc + term
            y = _silu(acc)
            if normalize:
                y = y * lax.rsqrt(jnp.sum(y * y, axis=-1, keepdims=True) + EPS)
                if scale != 1.0:
                    y = y * scale
            o_ref[0, pl.ds(t0, tile), lanes] = y.astype(o_ref.dtype)
            if emit_t:
                for h in range(tile // DN_PAIR):
                    ot_ref[0, hl, i * (tile // DN_PAIR) + h] = (
                        y[h * DN_PAIR:(h + 1) * DN_PAIR, :].T.astype(ot_ref.dtype))
        return carry

    lax.fori_loop(0, s // tile, body, 0)


def _dn_prep(proj, conv_w, col0, ncols, normalize, scale, emit_t):
    b, s, _ = proj.shape
    hd = DN_HEAD_DIM
    width = CONV_HEADS_PER_STEP * hd
    nh = ncols // hd
    cb = col0 // width
    out_shape = [jax.ShapeDtypeStruct((b, s, ncols), BF16)]
    out_specs = [pl.BlockSpec((1, s, width), lambda bi, h: (bi, 0, h))]
    if emit_t:
        out_shape.append(jax.ShapeDtypeStruct((b, nh, s // DN_PAIR, hd, DN_PAIR), BF16))
        out_specs.append(pl.BlockSpec((1, CONV_HEADS_PER_STEP, s // DN_PAIR, hd, DN_PAIR),
                                      lambda bi, h: (bi, h, 0, 0, 0)))
    res = pl.pallas_call(
        functools.partial(_dn_prep_kernel, normalize=normalize, scale=scale, emit_t=emit_t),
        grid=(b, ncols // width),
        in_specs=[pl.BlockSpec((1, s, width), lambda bi, h: (bi, 0, cb + h)),
                  pl.BlockSpec((DN_CONV_TAPS, width), lambda bi, h: (0, cb + h))],
        out_specs=out_specs,
        out_shape=out_shape,
        scratch_shapes=[pltpu.VMEM((CONV_HEADS_PER_STEP, s + 2 * CONV_PAD, hd), F32)],
        compiler_params=_params(("parallel", "parallel")),
        name="dn_conv",
    )(proj, conv_w)
    return res if emit_t else res[0]


def _dn_gates_kernel(ba_ref, alog_ref, dtb_ref, g_ref, beta_ref, gt_ref, et_ref):
    s = ba_ref.shape[1]
    nh2 = 2 * DN_V_HEADS
    ba = ba_ref[0]
    beta_ref[0] = jax.nn.sigmoid(ba[:, :nh2])
    z = ba[:, nh2:] + dtb_ref[...]
    softplus = jnp.maximum(z, 0.0) + jnp.log(1.0 + jnp.exp(-jnp.abs(z)))
    g = -jnp.exp(alog_ref[...]) * softplus
    r = lax.broadcasted_iota(jnp.int32, (DN_CHUNK, DN_CHUNK), 0)
    c = lax.broadcasted_iota(jnp.int32, (DN_CHUNK, DN_CHUNK), 1)
    lower = jnp.where(r >= c, 1.0, 0.0).astype(F32)
    upper = jnp.where(r <= c, 1.0, 0.0).astype(F32)
    is_fwd = lax.broadcasted_iota(jnp.int32, (DN_CHUNK, nh2), 1) < DN_V_HEADS
    for n in range(s // DN_CHUNK):
        gc = g[n * DN_CHUNK:(n + 1) * DN_CHUNK, :]
        pre = jnp.dot(lower, gc, preferred_element_type=F32, precision=lax.Precision.HIGHEST)
        suf = jnp.dot(upper, gc, preferred_element_type=F32, precision=lax.Precision.HIGHEST)
        g_ref[0, n * DN_CHUNK:(n + 1) * DN_CHUNK, :] = jnp.where(is_fwd, pre, suf)
        g_tot = jnp.sum(gc, axis=0, keepdims=True)
        gt_ref[0, n:n + 1, :] = g_tot
        et_ref[0, n:n + 1, :] = jnp.exp(g_tot)


def _dn_gates(ba, a_log, dt_bias):
    b, s, w = ba.shape
    nh2 = 2 * DN_V_HEADS
    nchunk = s // DN_CHUNK
    return pl.pallas_call(
        _dn_gates_kernel,
        grid=(b,),
        in_specs=[pl.BlockSpec((1, s, w), lambda bi: (bi, 0, 0)),
                  pl.BlockSpec((1, nh2), lambda bi: (0, 0)),
                  pl.BlockSpec((1, nh2), lambda bi: (0, 0))],
        out_specs=[pl.BlockSpec((1, s, nh2), lambda bi: (bi, 0, 0)),
                   pl.BlockSpec((1, s, nh2), lambda bi: (bi, 0, 0)),
                   pl.BlockSpec((1, nchunk, nh2), lambda bi: (bi, 0, 0)),
                   pl.BlockSpec((1, nchunk, nh2), lambda bi: (bi, 0, 0))],
        out_shape=[jax.ShapeDtypeStruct((b, s, nh2), F32),
                   jax.ShapeDtypeStruct((b, s, nh2), F32),
                   jax.ShapeDtypeStruct((b, nchunk, nh2), F32),
                   jax.ShapeDtypeStruct((b, nchunk, nh2), F32)],
        compiler_params=_params(("parallel",)),
        name="dn_gates",
    )(ba, a_log.reshape(1, nh2), dt_bias.reshape(1, nh2))


N_UNITS = 4
DN_HEADS_PER_STEP = 2
ROW_G, ROW_BETA, ROW_GT = 0, 4, 8


def _dn_core_kernel(q_ref, k_ref, kt_ref, v_ref, r_ref, et_ref, gate_ref, nw_ref, o_ref,
                    u_sc, w_sc, a_sc, qd_sc, kd_sc, st_ref, of_ref, ob_ref):
    s = q_ref.shape[1]
    hd = DN_HEAD_DIM
    pr = DN_PAIR
    npair = s // pr
    nchunk = 2 * npair

    frow = lax.broadcasted_iota(jnp.int32, (DN_CHUNK, pr), 0)
    fcol = lax.broadcasted_iota(jnp.int32, (DN_CHUNK, pr), 1)
    left = fcol < DN_CHUNK
    ccol = jnp.where(left, fcol, fcol - DN_CHUNK)
    masks = {False: (frow >= ccol, frow > ccol),
             True: (frow <= ccol, frow < ccol)}

    def fold(m):
        return jnp.where(left, m[:DN_CHUNK], m[DN_CHUNK:])

    zeros_half = jnp.zeros((DN_CHUNK, hd), F32)
    zeros_blk = jnp.zeros((DN_CHUNK, pr), BF16)
    zeros_rhs = jnp.zeros((DN_CHUNK, 2 * hd), BF16)
    eye_folded = jnp.where((fcol == frow) | (fcol == frow + DN_CHUNK), 1.0, 0.0).astype(F32)

    def mm(a, b):
        return jnp.dot(a, b, preferred_element_type=F32)

    nheads = DN_HEADS_PER_STEP
    half = npair // 2
    nw = nw_ref[...]

    def solve_pairs(pair_ids):
        heads = []
        for pi in pair_ids:
            t0 = pl.multiple_of(pi * pr, pr)
            for hl in range(nheads):
                heads.append((hl, pi, q_ref[0, pl.ds(t0, pr), hl * hd:(hl + 1) * hd],
                              k_ref[0, pl.ds(t0, pr), hl * hd:(hl + 1) * hd], kt_ref[0, hl, pi],
                              v_ref[0, pl.ds(t0, pr), 2 * hl * hd:2 * (hl + 1) * hd], r_ref[0, hl, pi]))
        yield
        both = [mm(jnp.concatenate([q2, k2], axis=0), kt2) for _, _, q2, k2, kt2, _, _ in heads]
        yield
        chains = []
        for (hl, pi, q2, k2, kt2, v2, rows), qkk in zip(heads, both):
            qk, kk = fold(qkk[:pr]), fold(qkk[pr:])
            q2f, k2f, kt2f = q2.astype(F32), k2.astype(F32), kt2.astype(F32)
            for sub in range(N_UNITS):
                unit = hl * N_UNITS + sub
                incl, strict = masks[sub >= 2]
                g_row = rows[ROW_G + sub:ROW_G + sub + 1]
                b_row = rows[ROW_BETA + sub:ROW_BETA + sub + 1]
                gt_row = rows[ROW_GT + sub:ROW_GT + sub + 1]
                m1 = jnp.broadcast_to(g_row, (pr, pr)).T
                diff = fold(m1) - g_row
                dm = jnp.where(incl, jnp.exp(jnp.where(incl, diff, 0.0)), 0.0)
                kt_mat = jnp.where(strict, kk * dm, 0.0) * b_row
                e1 = jnp.exp(m1)
                vh = sub % 2
                a_sc[unit, pi] = (qk * dm * b_row).astype(BF16)
                qd_sc[unit, pi] = (q2f * e1).astype(BF16)
                kd_sc[unit, pi] = (kt2f * (b_row * jnp.exp(gt_row - g_row))).astype(BF16)
                rhs = jnp.concatenate([v2[:, vh * hd:(vh + 1) * hd], (k2f * e1).astype(BF16)], axis=1)
                chains.append(dict(
                    unit=unit, pi=pi, p=-kt_mat, x=eye_folded,
                    rhs=jnp.concatenate(
                        [jnp.concatenate([rhs[DN_CHUNK:], zeros_rhs], axis=1),
                         jnp.concatenate([zeros_rhs, rhs[:DN_CHUNK]], axis=1)], axis=0)))
        for stage in range(6):
            outs = []
            for ch in chains:
                pb, xb = ch["p"].astype(BF16), ch["x"].astype(BF16)
                blk_a = jnp.where(left, pb, xb)
                blk_b = jnp.where(left, xb, pb)
                outs.append(mm(pb, jnp.concatenate(
                    [jnp.concatenate([blk_a, zeros_blk], axis=1),
                     jnp.concatenate([zeros_blk, blk_b], axis=1)], axis=0)))
            yield
            for ch, out in zip(chains, outs):
                ch["x"] = ch["x"] + jnp.where(left, out[:, pr:], out[:, :pr])
                if stage < 5:
                    ch["p"] = jnp.where(left, out[:, :pr], out[:, pr:])
        sols = [mm(ch["x"].astype(BF16), ch["rhs"]) for ch in chains]
        yield
        for ch, sol in zip(chains, sols):
            unit, pi = ch["unit"], ch["pi"]
            u_sc[unit, pi] = jnp.concatenate([sol[:, 2 * hd:3 * hd], sol[:, :hd]], axis=0)
            w_sc[unit, pi] = jnp.concatenate([sol[:, 3 * hd:], sol[:, hd:2 * hd]], axis=0).astype(BF16)

    def recur_step(it, finalize):
        units = []
        for unit in range(nheads * N_UNITS):
            hl, sub = divmod(unit, N_UNITS)
            backward = sub >= 2
            pi = (npair - 1 - it) if backward else it
            units.append(dict(
                unit=unit, backward=backward, pi=pi, vhead=2 * hl + sub % 2,
                u=u_sc[unit, pi], w=w_sc[unit, pi], a=a_sc[unit, pi],
                qd=qd_sc[unit, pi], kd=kd_sc[unit, pi], st=st_ref[unit], outs=[None, None],
                decay=[et_ref[0, hl, pl.ds(sub * nchunk + 2 * pi + c, 1), :] for c in range(2)]))
        if finalize:
            for un in units:
                t0 = pl.multiple_of(un["pi"] * pr, pr)
                lanes = slice(un["vhead"] * hd, (un["vhead"] + 1) * hd)
                if un["backward"]:
                    un["other"] = of_ref[un["vhead"], pl.ds(t0, pr), :]
                else:
                    un["other"] = ob_ref[un["vhead"], pl.ds(pl.multiple_of((un["pi"] - half) * pr, pr), pr), :]
                un["gate"] = gate_ref[0, pl.ds(t0, pr), lanes]
        yield
        for step in range(2):
            for un in units:
                c = (1 - step) if un["backward"] else step
                un["c"] = c
                un["rows"] = slice(c * DN_CHUNK, (c + 1) * DN_CHUNK)
            rss = [mm(jnp.concatenate([un["w"][un["rows"]], un["qd"][un["rows"]]], axis=0),
                      un["st"].astype(BF16)) for un in units]
            yield
            for un, rs in zip(units, rss):
                halves = [zeros_half, zeros_half]
                halves[un["c"]] = un["u"][un["rows"]] - rs[:DN_CHUNK]
                un["v_full"] = jnp.concatenate(halves, axis=0).astype(BF16)
                un["qs"] = rs[DN_CHUNK:]
            intra = [mm(un["a"], un["v_full"]) for un in units]
            upd = [mm(un["kd"], un["v_full"]) for un in units]
            yield
            for un, o_in, st_up in zip(units, intra, upd):
                un["outs"][un["c"]] = un["qs"] + o_in
                un["st"] = un["st"] * un["decay"][un["c"]] + st_up
        for un in units:
            t0 = pl.multiple_of(un["pi"] * pr, pr)
            o_pair = jnp.concatenate(un["outs"], axis=0)
            if finalize:
                o = o_pair + un["other"]
                o = o * lax.rsqrt(jnp.mean(o * o, axis=-1, keepdims=True) + EPS) * nw
                lanes = slice(un["vhead"] * hd, (un["vhead"] + 1) * hd)
                o_ref[0, pl.ds(t0, pr), lanes] = (o * _silu(un["gate"].astype(F32))).astype(o_ref.dtype)
            elif un["backward"]:
                ob_ref[un["vhead"], pl.ds(pl.multiple_of((un["pi"] - half) * pr, pr), pr), :] = o_pair
            else:
                of_ref[un["vhead"], pl.ds(t0, pr), :] = o_pair
            st_ref[un["unit"]] = un["st"]

    def run(*gens_and_order):
        gens, order = gens_and_order[:-1], gens_and_order[-1]
        for g in order:
            next(gens[g])
        for g in gens:
            for _ in g:
                pass

    MERGED_ORDER = (0, 1, 0, 1, 0, 0, 1, 0, 0, 1, 0, 0, 1, 0)

    st_ref[...] = jnp.zeros(st_ref.shape, F32)
    run(solve_pairs([0, npair - 1]), ())

    def merged(it, carry):
        run(solve_pairs([it + 1, npair - 2 - it]), recur_step(it, False), MERGED_ORDER)
        return carry

    lax.fori_loop(0, half - 1, merged, 0)
    run(recur_step(half - 1, False), ())

    def second_half(it, carry):
        run(recur_step(it, True), ())
        return carry

    lax.fori_loop(half, npair, second_half, 0)


def _dn_core(qn, kn, knt, vs, rows, etot, proj, norm_w, gate_col):
    b, s, _ = qn.shape
    hd = DN_HEAD_DIM
    npair = s // DN_PAIR
    nh = DN_HEADS_PER_STEP
    nu = nh * N_UNITS
    gb = gate_col // (2 * nh * hd)
    return pl.pallas_call(
        _dn_core_kernel,
        grid=(b, DN_QK_HEADS // nh),
        in_specs=[pl.BlockSpec((1, s, nh * hd), lambda bi, j: (bi, 0, j)),
                  pl.BlockSpec((1, s, nh * hd), lambda bi, j: (bi, 0, j)),
                  pl.BlockSpec((1, nh, npair, hd, DN_PAIR), lambda bi, j: (bi, j, 0, 0, 0)),
                  pl.BlockSpec((1, s, 2 * nh * hd), lambda bi, j: (bi, 0, j)),
                  pl.BlockSpec((1, nh, npair, 16, DN_PAIR), lambda bi, j: (bi, j, 0, 0, 0)),
                  pl.BlockSpec((1, nh, N_UNITS * 2 * npair, hd), lambda bi, j: (bi, j, 0, 0)),
                  pl.BlockSpec((1, s, 2 * nh * hd), lambda bi, j: (bi, 0, gb + j)),
                  pl.BlockSpec((1, hd), lambda bi, j: (0, 0))],
        out_specs=pl.BlockSpec((1, s, 2 * nh * hd), lambda bi, j: (bi, 0, j)),
        out_shape=jax.ShapeDtypeStruct((b, s, DN_V_WIDTH), BF16),
        scratch_shapes=[pltpu.VMEM((nu, npair, DN_PAIR, hd), F32),
                        pltpu.VMEM((nu, npair, DN_PAIR, hd), BF16),
                        pltpu.VMEM((nu, npair, DN_CHUNK, DN_PAIR), BF16),
                        pltpu.VMEM((nu, npair, DN_PAIR, hd), BF16),
                        pltpu.VMEM((nu, npair, hd, DN_PAIR), BF16),
                        pltpu.VMEM((nu, hd, hd), F32),
                        pltpu.VMEM((2 * nh, s // 2, hd), F32),
                        pltpu.VMEM((2 * nh, s // 2, hd), F32)],
        compiler_params=_params(("parallel", "parallel")),
        name="dn_core",
    )(qn, kn, knt, vs, rows, etot, proj, norm_w.reshape(1, hd))


def _per_unit(a):
    b, t, _ = a.shape
    a = a.reshape(b, t, 2, DN_QK_HEADS, 2)
    return a.transpose(0, 3, 2, 4, 1).reshape(b, DN_QK_HEADS, N_UNITS, t)


def _deltanet_mixer(proj, ba, conv_w, a_log, dt_bias, norm_w, gate_col):
    b, s, _ = proj.shape
    npair = s // DN_PAIR
    nchunk = s // DN_CHUNK
    qn = _dn_prep(proj, conv_w, 0, DN_QK_WIDTH, True, DN_HEAD_DIM ** -0.5, False)
    kn, knt = _dn_prep(proj, conv_w, DN_QK_WIDTH, DN_QK_WIDTH, True, 1.0, True)
    vs = _dn_prep(proj, conv_w, 2 * DN_QK_WIDTH, DN_V_WIDTH, False, 1.0, False)
    g_cum, beta, g_tot, e_tot = _dn_gates(ba, a_log, dt_bias)
    gt_tok = jnp.broadcast_to(g_tot[:, :, None, :], (b, nchunk, DN_CHUNK, g_tot.shape[-1])).reshape(b, s, -1)
    rows = jnp.concatenate([_per_unit(g_cum), _per_unit(beta), _per_unit(gt_tok),
                            jnp.zeros((b, DN_QK_HEADS, N_UNITS, s), F32)], axis=2)
    rows = rows.reshape(b, DN_QK_HEADS, 16, npair, DN_PAIR).transpose(0, 1, 3, 2, 4)
    etot = _per_unit(e_tot).reshape(b, DN_QK_HEADS, N_UNITS * nchunk)
    etot = jnp.broadcast_to(etot[..., None], (b, DN_QK_HEADS, N_UNITS * nchunk, DN_HEAD_DIM))
    return _dn_core(qn, kn, knt, vs, rows, etot, proj, norm_w, gate_col)


def kernel(x, mem, norm_w, mem_norm_w, w_kv_mem, w_out, pool_w_in, pool_w_group, pool_scale,
           dn_w_in, dn_conv_w, dn_a_log, dn_dt_bias, dn_norm_w, final_norm_w):
    b, s, d = x.shape
    m = mem.shape[1]
    depth = norm_w.shape[0]
    x2d = x.reshape(b * s, d)
    mem2d = mem.reshape(b * m, d)
    tm = min(2048, b * s)
    tr = min(512, b * s)
    for i in range(depth):
        j = i // 2
        hm = _rmsnorm(mem2d, mem_norm_w[i], BF16, min(512, b * m))
        kv = _proj(hm, w_kv_mem, i, 0, 2 * XA_WIDTH, BF16, min(1024, b * m), 1024)
        kv = kv.reshape(b, m, 2 * XA_WIDTH)
        h = _rmsnorm(x2d, norm_w[i], BF16, tr)
        if i % 2 == 0:
            proj = _proj(h, pool_w_in, j, 0, pool_w_in.shape[2], BF16, tm, 1024)
            proj = proj.reshape(b, s, -1)
            gate_col = POOL_WIDTH + XA_WIDTH
            ym = _pool_mix(proj, pool_w_group[j].astype(BF16), pool_scale[j], gate_col)
            ya = _xattn(proj, kv, POOL_WIDTH, gate_col + POOL_WIDTH, min(512, s))
        else:
            proj = _proj(h, dn_w_in, j, 0, DN_MAIN, BF16, tm, 1024)
            proj = proj.reshape(b, s, -1)
            ba = _proj(h, dn_w_in, j, DN_MAIN, 4 * DN_V_HEADS, F32, tm, 4 * DN_V_HEADS)
            ba = ba.reshape(b, s, -1)
            gate_col = DN_CONV_CH + XA_WIDTH
            ym = _deltanet_mixer(proj, ba, dn_conv_w[j], dn_a_log[j], dn_dt_bias[j], dn_norm_w[j], gate_col)
            ya = _xattn(proj, kv, DN_CONV_CH, gate_col + DN_V_WIDTH, min(512, s))
        x2d = _out_proj(ym.reshape(b * s, -1), ya.reshape(b * s, -1), w_out, i, x2d, min(1024, b * s), 512)
    return _rmsnorm(x2d, final_norm_w, F32, tr).reshape(b, s, d)
```

```python
import functools

import jax
import jax.numpy as jnp
from jax import lax
from jax.experimental import pallas as pl
from jax.experimental.pallas import tpu as pltpu

F32 = jnp.float32
BF16 = jnp.bfloat16

EPS = 1e-6
D_MODEL = 2048
N_MEM = 256
XA_HEADS = 4
XA_HEAD_DIM = 512
XA_WIDTH = XA_HEADS * XA_HEAD_DIM
POOL_GROUPS = 4
POOL_GROUP_WIDTH = 1024
POOL_WIDTH = POOL_GROUPS * POOL_GROUP_WIDTH
DN_QK_HEADS = 16
DN_V_HEADS = 32
DN_HEAD_DIM = 128
DN_QK_WIDTH = DN_QK_HEADS * DN_HEAD_DIM
DN_V_WIDTH = DN_V_HEADS * DN_HEAD_DIM
DN_CONV_CH = 2 * DN_QK_WIDTH + DN_V_WIDTH
DN_CONV_TAPS = 5
DN_CHUNK = 64
DN_PAIR = 2 * DN_CHUNK
INNER = POOL_WIDTH + XA_WIDTH
DN_MAIN = DN_CONV_CH + XA_WIDTH + INNER

VMEM_LIMIT = 56 * 1024 * 1024


def _params(sem):
    return pltpu.CompilerParams(dimension_semantics=sem, vmem_limit_bytes=VMEM_LIMIT)


def _silu(g):
    return g * jax.nn.sigmoid(g)


def _rmsnorm_kernel(x_ref, nw_ref, o_ref):
    xf = x_ref[...]
    ms = jnp.mean(xf * xf, axis=-1, keepdims=True)
    o_ref[...] = (xf * lax.rsqrt(ms + EPS) * nw_ref[...]).astype(o_ref.dtype)


def _rmsnorm(x2d, nw, out_dtype, tm):
    m, k = x2d.shape
    return pl.pallas_call(
        _rmsnorm_kernel,
        grid=(m // tm,),
        in_specs=[pl.BlockSpec((tm, k), lambda i: (i, 0)),
                  pl.BlockSpec((1, k), lambda i: (0, 0))],
        out_specs=pl.BlockSpec((tm, k), lambda i: (i, 0)),
        out_shape=jax.ShapeDtypeStruct((m, k), out_dtype),
        compiler_params=_params(("parallel",)),
        name="rmsnorm",
    )(x2d, nw.reshape(1, k))


def _proj_kernel(h_ref, w_ref, o_ref, wb_ref):
    @pl.when(pl.program_id(1) == 0)
    def _():
        wb_ref[...] = w_ref[...].astype(BF16)

    o_ref[...] = jnp.dot(h_ref[...], wb_ref[...], preferred_element_type=F32).astype(o_ref.dtype)


def _proj(h2d, w_stack, layer, col0, ncols, out_dtype, tm, tn):
    m, k = h2d.shape
    cb = col0 // tn
    return pl.pallas_call(
        _proj_kernel,
        grid=(ncols // tn, m // tm),
        in_specs=[pl.BlockSpec((tm, k), lambda j, i: (i, 0)),
                  pl.BlockSpec((None, k, tn), lambda j, i: (layer, 0, cb + j))],
        out_specs=pl.BlockSpec((tm, tn), lambda j, i: (i, j)),
        out_shape=jax.ShapeDtypeStruct((m, ncols), out_dtype),
        scratch_shapes=[pltpu.VMEM((k, tn), BF16)],
        compiler_params=_params(("parallel", "arbitrary")),
        name="proj",
    )(h2d, w_stack)


def _out_proj_kernel(ym_ref, ya_ref, w1_ref, w2_ref, x_ref, o_ref, w1b_ref, w2b_ref):
    @pl.when(pl.program_id(1) == 0)
    def _():
        w1b_ref[...] = w1_ref[...].astype(BF16)
        w2b_ref[...] = w2_ref[...].astype(BF16)

    acc = jnp.dot(ym_ref[...], w1b_ref[...], preferred_element_type=F32)
    acc = acc + jnp.dot(ya_ref[...], w2b_ref[...], preferred_element_type=F32)
    o_ref[...] = x_ref[...] + acc


def _out_proj(ym, ya, w_stack, layer, x2d, tm, tn):
    m, n = x2d.shape
    k1, k2 = ym.shape[1], ya.shape[1]
    assert k1 % k2 == 0
    return pl.pallas_call(
        _out_proj_kernel,
        grid=(n // tn, m // tm),
        in_specs=[pl.BlockSpec((tm, k1), lambda j, i: (i, 0)),
                  pl.BlockSpec((tm, k2), lambda j, i: (i, 0)),
                  pl.BlockSpec((None, k1, tn), lambda j, i: (layer, 0, j), pipeline_mode=pl.Buffered(1)),
                  pl.BlockSpec((None, k2, tn), lambda j, i: (layer, k1 // k2, j), pipeline_mode=pl.Buffered(1)),
                  pl.BlockSpec((tm, tn), lambda j, i: (i, j))],
        out_specs=pl.BlockSpec((tm, tn), lambda j, i: (i, j)),
        out_shape=jax.ShapeDtypeStruct((m, n), F32),
        scratch_shapes=[pltpu.VMEM((k1, tn), BF16), pltpu.VMEM((k2, tn), BF16)],
        compiler_params=_params(("parallel", "arbitrary")),
        name="out_proj",
    )(ym, ya, w_stack, w_stack, x2d)


def _xattn_kernel(q_ref, k_ref, v_ref, g_ref, o_ref):
    hd = XA_HEAD_DIM
    heads = [slice(h * hd, (h + 1) * hd) for h in range(XA_HEADS)]
    scores = [lax.dot_general(q_ref[0, :, hs], k_ref[0, :, hs], (((1,), (1,)), ((), ())),
                              preferred_element_type=F32) for hs in heads]
    probs, denoms = [], []
    for s in scores:
        s = s * (hd ** -0.5)
        p = jnp.exp(s - jnp.max(s, axis=-1, keepdims=True))
        denoms.append(jnp.sum(p, axis=-1, keepdims=True))
        probs.append(p.astype(BF16))
    outs = [jnp.dot(p, v_ref[0, :, hs], preferred_element_type=F32) for p, hs in zip(probs, heads)]
    for o, l, hs in zip(outs, denoms, heads):
        o_ref[0, :, hs] = (o / l * _silu(g_ref[0, :, hs].astype(F32))).astype(o_ref.dtype)


def _xattn(proj, kv, q_col, gate_col, ts):
    b, s, _ = proj.shape
    w = XA_WIDTH
    qb, gb = q_col // w, gate_col // w
    return pl.pallas_call(
        _xattn_kernel,
        grid=(b, s // ts),
        in_specs=[pl.BlockSpec((1, ts, w), lambda bi, t: (bi, t, qb)),
                  pl.BlockSpec((1, N_MEM, w), lambda bi, t: (bi, 0, 0)),
                  pl.BlockSpec((1, N_MEM, w), lambda bi, t: (bi, 0, 1)),
                  pl.BlockSpec((1, ts, w), lambda bi, t: (bi, t, gb))],
        out_specs=pl.BlockSpec((1, ts, w), lambda bi, t: (bi, t, 0)),
        out_shape=jax.ShapeDtypeStruct((b, s, XA_WIDTH), BF16),
        compiler_params=_params(("parallel", "parallel")),
        name="mem_xattn",
    )(proj, kv, kv, proj)


POOL_TILE = 128
POOL_SUBTILES = 4
POOL_HALO = 64


def _pool_kernel(u_ref, g_ref, wg_ref, sc_ref, o_ref, ext_ref):
    s = u_ref.shape[1]
    gw = u_ref.shape[2]
    grp = pl.program_id(1)
    half = jnp.left_shift(1, grp)
    ext_ref[0:POOL_HALO, :] = jnp.zeros((POOL_HALO, gw), BF16)
    ext_ref[POOL_HALO + s:POOL_HALO + s + POOL_HALO, :] = jnp.zeros((POOL_HALO, gw), BF16)
    ext_ref[POOL_HALO:POOL_HALO + s, :] = u_ref[0]

    kk = POOL_TILE + 2 * POOL_HALO
    r = lax.broadcasted_iota(jnp.int32, (POOL_TILE, kk), 0)
    c = lax.broadcasted_iota(jnp.int32, (POOL_TILE, kk), 1)
    d = c - POOL_HALO - r
    band = jnp.where((d >= -half) & (d < half), 1.0, 0.0).astype(BF16)
    wg = wg_ref[0]
    scale = sc_ref[0]

    rows = POOL_SUBTILES * POOL_TILE

    def body(i, carry):
        base = pl.multiple_of(i * rows, rows)
        pooled = []
        for sub in range(POOL_SUBTILES):
            t0 = base + sub * POOL_TILE
            wsum = jnp.dot(band, ext_ref[pl.ds(t0, kk), :], preferred_element_type=F32)
            tok = t0 + lax.broadcasted_iota(jnp.int32, (POOL_TILE, gw), 0)
            cnt = (jnp.minimum(tok + half, s) - jnp.maximum(tok - half, 0)).astype(F32)
            centre = ext_ref[pl.ds(t0 + POOL_HALO, POOL_TILE), :].astype(F32)
            pooled.append((wsum / cnt - centre).astype(BF16))
        group = 2
        mixed = [jnp.dot(jnp.concatenate(pooled[k:k + group], axis=0), wg, preferred_element_type=F32)
                 for k in range(0, POOL_SUBTILES, group)]
        for k, mx in enumerate(mixed):
            r0 = base + k * group * POOL_TILE
            gate = g_ref[0, pl.ds(r0, group * POOL_TILE), :].astype(F32)
            o_ref[0, pl.ds(r0, group * POOL_TILE), :] = (mx * scale * _silu(gate)).astype(o_ref.dtype)
        return carry

    lax.fori_loop(0, s // rows, body, 0)


def _pool_mix(proj, wg, scale, gate_col):
    b, s, _ = proj.shape
    gw = POOL_GROUP_WIDTH
    gb = gate_col // gw
    return pl.pallas_call(
        _pool_kernel,
        grid=(b, POOL_GROUPS),
        in_specs=[pl.BlockSpec((1, s, gw), lambda bi, g: (bi, 0, g)),
                  pl.BlockSpec((1, s, gw), lambda bi, g: (bi, 0, gb + g)),
                  pl.BlockSpec((1, gw, gw), lambda bi, g: (g, 0, 0)),
                  pl.BlockSpec((1, 1, gw), lambda bi, g: (g, 0, 0))],
        out_specs=pl.BlockSpec((1, s, gw), lambda bi, g: (bi, 0, g)),
        out_shape=jax.ShapeDtypeStruct((b, s, POOL_WIDTH), BF16),
        scratch_shapes=[pltpu.VMEM((s + 2 * POOL_HALO, gw), BF16)],
        compiler_params=_params(("parallel", "parallel")),
        name="pool_mix",
    )(proj, proj, wg, scale.reshape(POOL_GROUPS, 1, gw))


CONV_PAD = 8
CONV_HEADS_PER_STEP = 4


def _dn_prep_kernel(x_ref, cw_ref, *refs, normalize, scale, emit_t):
    if emit_t:
        o_ref, ot_ref, ext_ref = refs
    else:
        o_ref, ext_ref = refs
    s = x_ref.shape[1]
    width = x_ref.shape[2]
    hd = DN_HEAD_DIM
    for hl in range(width // hd):
        ext_ref[hl, 0:CONV_PAD, :] = jnp.zeros((CONV_PAD, hd), F32)
        ext_ref[hl, CONV_PAD + s:2 * CONV_PAD + s, :] = jnp.zeros((CONV_PAD, hd), F32)
        ext_ref[hl, CONV_PAD:CONV_PAD + s, :] = x_ref[0, :, hl * hd:(hl + 1) * hd].astype(F32)
    cw = cw_ref[...]
    tile = min(s, 256)

    def body(i, carry):
        t0 = pl.multiple_of(i * tile, tile)
        for hl in range(width // hd):
            lanes = slice(hl * hd, (hl + 1) * hd)
            acc = None
            for tap in range(DN_CONV_TAPS):
                off = CONV_PAD + tap - DN_CONV_TAPS // 2
                term = ext_ref[hl, pl.ds(t0 + off, tile), :] * cw[tap:tap + 1, lanes]
                acc = term if acc is None else acc + term
            y = _silu(acc)
            if normalize:
                y = y * lax.rsqrt(jnp.sum(y * y, axis=-1, keepdims=True) + EPS)
                if scale != 1.0:
                    y = y * scale
            o_ref[0, pl.ds(t0, tile), lanes] = y.astype(o_ref.dtype)
            if emit_t:
                for h in range(tile // DN_PAIR):
                    ot_ref[0, hl, i * (tile // DN_PAIR) + h] = (
                        y[h * DN_PAIR:(h + 1) * DN_PAIR, :].T.astype(ot_ref.dtype))
        return carry

    lax.fori_loop(0, s // tile, body, 0)


def _dn_prep(proj, conv_w, col0, ncols, normalize, scale, emit_t):
    b, s, _ = proj.shape
    hd = DN_HEAD_DIM
    width = CONV_HEADS_PER_STEP * hd
    nh = ncols // hd
    cb = col0 // width
    out_shape = [jax.ShapeDtypeStruct((b, s, ncols), BF16)]
    out_specs = [pl.BlockSpec((1, s, width), lambda bi, h: (bi, 0, h))]
    if emit_t:
        out_shape.append(jax.ShapeDtypeStruct((b, nh, s // DN_PAIR, hd, DN_PAIR), BF16))
        out_specs.append(pl.BlockSpec((1, CONV_HEADS_PER_STEP, s // DN_PAIR, hd, DN_PAIR),
                                      lambda bi, h: (bi, h, 0, 0, 0)))
    res = pl.pallas_call(
        functools.partial(_dn_prep_kernel, normalize=normalize, scale=scale, emit_t=emit_t),
        grid=(b, ncols // width),
        in_specs=[pl.BlockSpec((1, s, width), lambda bi, h: (bi, 0, cb + h)),
                  pl.BlockSpec((DN_CONV_TAPS, width), lambda bi, h: (0, cb + h))],
        out_specs=out_specs,
        out_shape=out_shape,
        scratch_shapes=[pltpu.VMEM((CONV_HEADS_PER_STEP, s + 2 * CONV_PAD, hd), F32)],
        compiler_params=_params(("parallel", "parallel")),
        name="dn_conv",
    )(proj, conv_w)
    return res if emit_t else res[0]


def _dn_gates_kernel(ba_ref, alog_ref, dtb_ref, g_ref, beta_ref, gt_ref, et_ref):
    s = ba_ref.shape[1]
    nh2 = 2 * DN_V_HEADS
    ba = ba_ref[0]
    beta_ref[0] = jax.nn.sigmoid(ba[:, :nh2])
    z = ba[:, nh2:] + dtb_ref[...]
    softplus = jnp.maximum(z, 0.0) + jnp.log(1.0 + jnp.exp(-jnp.abs(z)))
    g = -jnp.exp(alog_ref[...]) * softplus
    r = lax.broadcasted_iota(jnp.int32, (DN_CHUNK, DN_CHUNK), 0)
    c = lax.broadcasted_iota(jnp.int32, (DN_CHUNK, DN_CHUNK), 1)
    lower = jnp.where(r >= c, 1.0, 0.0).astype(F32)
    upper = jnp.where(r <= c, 1.0, 0.0).astype(F32)
    is_fwd = lax.broadcasted_iota(jnp.int32, (DN_CHUNK, nh2), 1) < DN_V_HEADS
    for n in range(s // DN_CHUNK):
        gc = g[n * DN_CHUNK:(n + 1) * DN_CHUNK, :]
        pre = jnp.dot(lower, gc, preferred_element_type=F32, precision=lax.Precision.HIGHEST)
        suf = jnp.dot(upper, gc, preferred_element_type=F32, precision=lax.Precision.HIGHEST)
        g_ref[0, n * DN_CHUNK:(n + 1) * DN_CHUNK, :] = jnp.where(is_fwd, pre, suf)
        g_tot = jnp.sum(gc, axis=0, keepdims=True)
        gt_ref[0, n:n + 1, :] = g_tot
        et_ref[0, n:n + 1, :] = jnp.exp(g_tot)


def _dn_gates(ba, a_log, dt_bias):
    b, s, w = ba.shape
    nh2 = 2 * DN_V_HEADS
    nchunk = s // DN_CHUNK
    return pl.pallas_call(
        _dn_gates_kernel,
        grid=(b,),
        in_specs=[pl.BlockSpec((1, s, w), lambda bi: (bi, 0, 0)),
                  pl.BlockSpec((1, nh2), lambda bi: (0, 0)),
                  pl.BlockSpec((1, nh2), lambda bi: (0, 0))],
        out_specs=[pl.BlockSpec((1, s, nh2), lambda bi: (bi, 0, 0)),
                   pl.BlockSpec((1, s, nh2), lambda bi: (bi, 0, 0)),
                   pl.BlockSpec((1, nchunk, nh2), lambda bi: (bi, 0, 0)),
                   pl.BlockSpec((1, nchunk, nh2), lambda bi: (bi, 0, 0))],
        out_shape=[jax.ShapeDtypeStruct((b, s, nh2), F32),
                   jax.ShapeDtypeStruct((b, s, nh2), F32),
                   jax.ShapeDtypeStruct((b, nchunk, nh2), F32),
                   jax.ShapeDtypeStruct((b, nchunk, nh2), F32)],
        compiler_params=_params(("parallel",)),
        name="dn_gates",
    )(ba, a_log.reshape(1, nh2), dt_bias.reshape(1, nh2))


N_UNITS = 4
DN_HEADS_PER_STEP = 2
ROW_G, ROW_BETA, ROW_GT = 0, 4, 8


def _dn_core_kernel(q_ref, k_ref, kt_ref, v_ref, r_ref, et_ref, gate_ref, nw_ref, o_ref,
                    u_sc, w_sc, a_sc, qd_sc, kd_sc, st_ref, of_ref, ob_ref):
    s = q_ref.shape[1]
    hd = DN_HEAD_DIM
    pr = DN_PAIR
    npair = s // pr
    nchunk = 2 * npair

    frow = lax.broadcasted_iota(jnp.int32, (DN_CHUNK, pr), 0)
    fcol = lax.broadcasted_iota(jnp.int32, (DN_CHUNK, pr), 1)
    left = fcol < DN_CHUNK
    ccol = jnp.where(left, fcol, fcol - DN_CHUNK)
    masks = {False: (frow >= ccol, frow > ccol),
             True: (frow <= ccol, frow < ccol)}

    def fold(m):
        return jnp.where(left, m[:DN_CHUNK], m[DN_CHUNK:])

    zeros_half = jnp.zeros((DN_CHUNK, hd), F32)
    zeros_blk = jnp.zeros((DN_CHUNK, pr), BF16)
    zeros_rhs = jnp.zeros((DN_CHUNK, 2 * hd), BF16)
    eye_folded = jnp.where((fcol == frow) | (fcol == frow + DN_CHUNK), 1.0, 0.0).astype(F32)

    def mm(a, b):
        return jnp.dot(a, b, preferred_element_type=F32)

    nheads = DN_HEADS_PER_STEP
    half = npair // 2
    nw = nw_ref[...]

    FWD, BWD = (0, 1), (2, 3)

    def solve_pairs(specs):
        heads = []
        for pi, subs in specs:
            t0 = pl.multiple_of(pi * pr, pr)
            for hl in range(nheads):
                heads.append((hl, pi, subs, q_ref[0, pl.ds(t0, pr), hl * hd:(hl + 1) * hd],
                              k_ref[0, pl.ds(t0, pr), hl * hd:(hl + 1) * hd], kt_ref[0, hl, pi],
                              v_ref[0, pl.ds(t0, pr), 2 * hl * hd:2 * (hl + 1) * hd], r_ref[0, hl, pi]))
        yield
        both = [mm(jnp.concatenate([q2, k2], axis=0), kt2) for _, _, _, q2, k2, kt2, _, _ in heads]
        yield
        chains = []
        for (hl, pi, subs, q2, k2, kt2, v2, rows), qkk in zip(heads, both):
            qk, kk = fold(qkk[:pr]), fold(qkk[pr:])
            q2f, k2f, kt2f = q2.astype(F32), k2.astype(F32), kt2.astype(F32)
            for sub in subs:
                unit = hl * N_UNITS + sub
                incl, strict = masks[sub >= 2]
                g_row = rows[ROW_G + sub:ROW_G + sub + 1]
                b_row = rows[ROW_BETA + sub:ROW_BETA + sub + 1]
                gt_row = rows[ROW_GT + sub:ROW_GT + sub + 1]
                m1 = jnp.broadcast_to(g_row, (pr, pr)).T
                diff = fold(m1) - g_row
                dm = jnp.where(incl, jnp.exp(jnp.where(incl, diff, 0.0)), 0.0)
                kt_mat = jnp.where(strict, kk * dm, 0.0) * b_row
                e1 = jnp.exp(m1)
                vh = sub % 2
                a_sc[unit, pi] = (qk * dm * b_row).astype(BF16)
                qd_sc[unit, pi] = (q2f * e1).astype(BF16)
                kd_sc[unit, pi] = (kt2f * (b_row * jnp.exp(gt_row - g_row))).astype(BF16)
                rhs = jnp.concatenate([v2[:, vh * hd:(vh + 1) * hd], (k2f * e1).astype(BF16)], axis=1)
                chains.append(dict(
                    unit=unit, pi=pi, p=-kt_mat, x=eye_folded,
                    rhs=jnp.concatenate(
                        [jnp.concatenate([rhs[DN_CHUNK:], zeros_rhs], axis=1),
                         jnp.concatenate([zeros_rhs, rhs[:DN_CHUNK]], axis=1)], axis=0)))
        for stage in range(6):
            outs = []
            for ch in chains:
                pb, xb = ch["p"].astype(BF16), ch["x"].astype(BF16)
                blk_a = jnp.where(left, pb, xb)
                blk_b = jnp.where(left, xb, pb)
                outs.append(mm(pb, jnp.concatenate(
                    [jnp.concatenate([blk_a, zeros_blk], axis=1),
                     jnp.concatenate([zeros_blk, blk_b], axis=1)], axis=0)))
            yield
            for ch, out in zip(chains, outs):
                ch["x"] = ch["x"] + jnp.where(left, out[:, pr:], out[:, :pr])
                if stage < 5:
                    ch["p"] = jnp.where(left, out[:, :pr], out[:, pr:])
        sols = [mm(ch["x"].astype(BF16), ch["rhs"]) for ch in chains]
        yield
        for ch, sol in zip(chains, sols):
            unit, pi = ch["unit"], ch["pi"]
            u_sc[unit, pi] = jnp.concatenate([sol[:, 2 * hd:3 * hd], sol[:, :hd]], axis=0)
            w_sc[unit, pi] = jnp.concatenate([sol[:, 3 * hd:], sol[:, hd:2 * hd]], axis=0).astype(BF16)

    def recur_step(it, finalize):
        units = []
        for unit in range(nheads * N_UNITS):
            hl, sub = divmod(unit, N_UNITS)
            backward = sub >= 2
            pi = (npair - 1 - it) if backward else it
            units.append(dict(
                unit=unit, backward=backward, pi=pi, vhead=2 * hl + sub % 2,
                u=u_sc[unit, pi], w=w_sc[unit, pi], a=a_sc[unit, pi],
                qd=qd_sc[unit, pi], kd=kd_sc[unit, pi], st=st_ref[unit], outs=[None, None],
                decay=[et_ref[0, hl, pl.ds(sub * nchunk + 2 * pi + c, 1), :] for c in range(2)]))
        if finalize:
            for un in units:
                t0 = pl.multiple_of(un["pi"] * pr, pr)
                lanes = slice(un["vhead"] * hd, (un["vhead"] + 1) * hd)
                if un["backward"]:
                    un["other"] = of_ref[un["vhead"], pl.ds(t0, pr), :]
                else:
                    un["other"] = ob_ref[un["vhead"], pl.ds(pl.multiple_of((un["pi"] - half) * pr, pr), pr), :]
                un["gate"] = gate_ref[0, pl.ds(t0, pr), lanes]
        yield
        for step in range(2):
            for un in units:
                c = (1 - step) if un["backward"] else step
                un["c"] = c
                un["rows"] = slice(c * DN_CHUNK, (c + 1) * DN_CHUNK)
            rss = [mm(jnp.concatenate([un["w"][un["rows"]], un["qd"][un["rows"]]], axis=0),
                      un["st"].astype(BF16)) for un in units]
            yield
            for un, rs in zip(units, rss):
                halves = [zeros_half, zeros_half]
                halves[un["c"]] = un["u"][un["rows"]] - rs[:DN_CHUNK]
                un["v_full"] = jnp.concatenate(halves, axis=0).astype(BF16)
                un["qs"] = rs[DN_CHUNK:]
            intra = [mm(un["a"], un["v_full"]) for un in units]
            upd = [mm(un["kd"], un["v_full"]) for un in units]
            yield
            for un, o_in, st_up in zip(units, intra, upd):
                un["outs"][un["c"]] = un["qs"] + o_in
                un["st"] = un["st"] * un["decay"][un["c"]] + st_up
        for un in units:
            t0 = pl.multiple_of(un["pi"] * pr, pr)
            o_pair = jnp.concatenate(un["outs"], axis=0)
            if finalize:
                o = o_pair + un["other"]
                o = o * lax.rsqrt(jnp.mean(o * o, axis=-1, keepdims=True) + EPS) * nw
                lanes = slice(un["vhead"] * hd, (un["vhead"] + 1) * hd)
                o_ref[0, pl.ds(t0, pr), lanes] = (o * _silu(un["gate"].astype(F32))).astype(o_ref.dtype)
            elif un["backward"]:
                ob_ref[un["vhead"], pl.ds(pl.multiple_of((un["pi"] - half) * pr, pr), pr), :] = o_pair
            else:
                of_ref[un["vhead"], pl.ds(t0, pr), :] = o_pair
            st_ref[un["unit"]] = un["st"]

    def drain(gen):
        for _ in gen:
            pass

    def ahead(t):
        return [(2 * t, FWD), (2 * t + 1, FWD), (npair - 1 - 2 * t, BWD), (npair - 2 - 2 * t, BWD)]

    assert half % 2 == 0
    st_ref[...] = jnp.zeros(st_ref.shape, F32)
    drain(solve_pairs(ahead(0)))

    def merged(t, carry, finalize):
        solve = solve_pairs(ahead(t + 1))
        first, second = recur_step(2 * t, finalize), recur_step(2 * t + 1, finalize)
        next(solve)
        for _ in range(4):
            next(first)
            next(solve)
        next(first)
        drain(first)
        for _ in range(4):
            next(second)
            next(solve)
        next(second)
        drain(solve)
        drain(second)
        return carry

    lax.fori_loop(0, half // 2, functools.partial(merged, finalize=False), 0)
    lax.fori_loop(half // 2, npair // 2 - 1, functools.partial(merged, finalize=True), 0)
    drain(recur_step(npair - 2, True))
    drain(recur_step(npair - 1, True))


def _dn_core(qn, kn, knt, vs, rows, etot, proj, norm_w, gate_col):
    b, s, _ = qn.shape
    hd = DN_HEAD_DIM
    npair = s // DN_PAIR
    nh = DN_HEADS_PER_STEP
    nu = nh * N_UNITS
    gb = gate_col // (2 * nh * hd)
    return pl.pallas_call(
        _dn_core_kernel,
        grid=(b, DN_QK_HEADS // nh),
        in_specs=[pl.BlockSpec((1, s, nh * hd), lambda bi, j: (bi, 0, j)),
                  pl.BlockSpec((1, s, nh * hd), lambda bi, j: (bi, 0, j)),
                  pl.BlockSpec((1, nh, npair, hd, DN_PAIR), lambda bi, j: (bi, j, 0, 0, 0)),
                  pl.BlockSpec((1, s, 2 * nh * hd), lambda bi, j: (bi, 0, j)),
                  pl.BlockSpec((1, nh, npair, 16, DN_PAIR), lambda bi, j: (bi, j, 0, 0, 0)),
                  pl.BlockSpec((1, nh, N_UNITS * 2 * npair, hd), lambda bi, j: (bi, j, 0, 0)),
                  pl.BlockSpec((1, s, 2 * nh * hd), lambda bi, j: (bi, 0, gb + j)),
                  pl.BlockSpec((1, hd), lambda bi, j: (0, 0))],
        out_specs=pl.BlockSpec((1, s, 2 * nh * hd), lambda bi, j: (bi, 0, j)),
        out_shape=jax.ShapeDtypeStruct((b, s, DN_V_WIDTH), BF16),
        scratch_shapes=[pltpu.VMEM((nu, npair, DN_PAIR, hd), F32),
                        pltpu.VMEM((nu, npair, DN_PAIR, hd), BF16),
                        pltpu.VMEM((nu, npair, DN_CHUNK, DN_PAIR), BF16),
                        pltpu.VMEM((nu, npair, DN_PAIR, hd), BF16),
                        pltpu.VMEM((nu, npair, hd, DN_PAIR), BF16),
                        pltpu.VMEM((nu, hd, hd), F32),
                        pltpu.VMEM((2 * nh, s // 2, hd), F32),
                        pltpu.VMEM((2 * nh, s // 2, hd), F32)],
        compiler_params=_params(("parallel", "parallel")),
        name="dn_core",
    )(qn, kn, knt, vs, rows, etot, proj, norm_w.reshape(1, hd))


def _per_unit(a):
    b, t, _ = a.shape
    a = a.reshape(b, t, 2, DN_QK_HEADS, 2)
    return a.transpose(0, 3, 2, 4, 1).reshape(b, DN_QK_HEADS, N_UNITS, t)


def _deltanet_mixer(proj, ba, conv_w, a_log, dt_bias, norm_w, gate_col):
    b, s, _ = proj.shape
    npair = s // DN_PAIR
    nchunk = s // DN_CHUNK
    qn = _dn_prep(proj, conv_w, 0, DN_QK_WIDTH, True, DN_HEAD_DIM ** -0.5, False)
    kn, knt = _dn_prep(proj, conv_w, DN_QK_WIDTH, DN_QK_WIDTH, True, 1.0, True)
    vs = _dn_prep(proj, conv_w, 2 * DN_QK_WIDTH, DN_V_WIDTH, False, 1.0, False)
    g_cum, beta, g_tot, e_tot = _dn_gates(ba, a_log, dt_bias)
    gt_tok = jnp.broadcast_to(g_tot[:, :, None, :], (b, nchunk, DN_CHUNK, g_tot.shape[-1])).reshape(b, s, -1)
    rows = jnp.concatenate([_per_unit(g_cum), _per_unit(beta), _per_unit(gt_tok),
                            jnp.zeros((b, DN_QK_HEADS, N_UNITS, s), F32)], axis=2)
    rows = rows.reshape(b, DN_QK_HEADS, 16, npair, DN_PAIR).transpose(0, 1, 3, 2, 4)
    etot = _per_unit(e_tot).reshape(b, DN_QK_HEADS, N_UNITS * nchunk)
    etot = jnp.broadcast_to(etot[..., None], (b, DN_QK_HEADS, N_UNITS * nchunk, DN_HEAD_DIM))
    return _dn_core(qn, kn, knt, vs, rows, etot, proj, norm_w, gate_col)


def kernel(x, mem, norm_w, mem_norm_w, w_kv_mem, w_out, pool_w_in, pool_w_group, pool_scale,
           dn_w_in, dn_conv_w, dn_a_log, dn_dt_bias, dn_norm_w, final_norm_w):
    b, s, d = x.shape
    m = mem.shape[1]
    depth = norm_w.shape[0]
    x2d = x.reshape(b * s, d)
    mem2d = mem.reshape(b * m, d)
    tm = min(2048, b * s)
    tr = min(512, b * s)
    for i in range(depth):
        j = i // 2
        hm = _rmsnorm(mem2d, mem_norm_w[i], BF16, min(512, b * m))
        kv = _proj(hm, w_kv_mem, i, 0, 2 * XA_WIDTH, BF16, min(1024, b * m), 1024)
        kv = kv.reshape(b, m, 2 * XA_WIDTH)
        h = _rmsnorm(x2d, norm_w[i], BF16, tr)
        if i % 2 == 0:
            proj = _proj(h, pool_w_in, j, 0, pool_w_in.shape[2], BF16, tm, 1024)
            proj = proj.reshape(b, s, -1)
            gate_col = POOL_WIDTH + XA_WIDTH
            ym = _pool_mix(proj, pool_w_group[j].astype(BF16), pool_scale[j], gate_col)
            ya = _xattn(proj, kv, POOL_WIDTH, gate_col + POOL_WIDTH, min(512, s))
        else:
            proj = _proj(h, dn_w_in, j, 0, DN_MAIN, BF16, tm, 1024)
            proj = proj.reshape(b, s, -1)
            ba = _proj(h, dn_w_in, j, DN_MAIN, 4 * DN_V_HEADS, F32, tm, 4 * DN_V_HEADS)
            ba = ba.reshape(b, s, -1)
            gate_col = DN_CONV_CH + XA_WIDTH
            ym = _deltanet_mixer(proj, ba, dn_conv_w[j], dn_a_log[j], dn_dt_bias[j], dn_norm_w[j], gate_col)
            ya = _xattn(proj, kv, DN_CONV_CH, gate_col + DN_V_WIDTH, min(512, s))
        x2d = _out_proj(ym.reshape(b * s, -1), ya.reshape(b * s, -1), w_out, i, x2d, min(1024, b * s), 512)
    return _rmsnorm(x2d, final_norm_w, F32, tr).reshape(b, s, d)
```

```python
import functools

import jax
import jax.numpy as jnp
from jax import lax
from jax.experimental import pallas as pl
from jax.experimental.pallas import tpu as pltpu

F32 = jnp.float32
BF16 = jnp.bfloat16

EPS = 1e-6
D_MODEL = 2048
N_MEM = 256
XA_HEADS = 4
XA_HEAD_DIM = 512
XA_WIDTH = XA_HEADS * XA_HEAD_DIM
POOL_GROUPS = 4
POOL_GROUP_WIDTH = 1024
POOL_WIDTH = POOL_GROUPS * POOL_GROUP_WIDTH
DN_QK_HEADS = 16
DN_V_HEADS = 32
DN_HEAD_DIM = 128
DN_QK_WIDTH = DN_QK_HEADS * DN_HEAD_DIM
DN_V_WIDTH = DN_V_HEADS * DN_HEAD_DIM
DN_CONV_CH = 2 * DN_QK_WIDTH + DN_V_WIDTH
DN_CONV_TAPS = 5
DN_CHUNK = 64
DN_PAIR = 2 * DN_CHUNK
INNER = POOL_WIDTH + XA_WIDTH
DN_MAIN = DN_CONV_CH + XA_WIDTH + INNER

VMEM_LIMIT = 56 * 1024 * 1024


def _params(sem):
    return pltpu.CompilerParams(dimension_semantics=sem, vmem_limit_bytes=VMEM_LIMIT)


def _silu(g):
    return g * jax.nn.sigmoid(g)


def _rmsnorm_kernel(x_ref, nw_ref, o_ref):
    xf = x_ref[...]
    ms = jnp.mean(xf * xf, axis=-1, keepdims=True)
    o_ref[...] = (xf * lax.rsqrt(ms + EPS) * nw_ref[...]).astype(o_ref.dtype)


def _rmsnorm(x2d, nw, out_dtype, tm):
    m, k = x2d.shape
    return pl.pallas_call(
        _rmsnorm_kernel,
        grid=(m // tm,),
        in_specs=[pl.BlockSpec((tm, k), lambda i: (i, 0)),
                  pl.BlockSpec((1, k), lambda i: (0, 0))],
        out_specs=pl.BlockSpec((tm, k), lambda i: (i, 0)),
        out_shape=jax.ShapeDtypeStruct((m, k), out_dtype),
        compiler_params=_params(("parallel",)),
        name="rmsnorm",
    )(x2d, nw.reshape(1, k))


def _proj_kernel(h_ref, w_ref, o_ref, wb_ref):
    @pl.when(pl.program_id(1) == 0)
    def _():
        wb_ref[...] = w_ref[...].astype(BF16)

    o_ref[...] = jnp.dot(h_ref[...], wb_ref[...], preferred_element_type=F32).astype(o_ref.dtype)


def _proj(h2d, w_stack, layer, col0, ncols, out_dtype, tm, tn):
    m, k = h2d.shape
    cb = col0 // tn
    return pl.pallas_call(
        _proj_kernel,
        grid=(ncols // tn, m // tm),
        in_specs=[pl.BlockSpec((tm, k), lambda j, i: (i, 0)),
                  pl.BlockSpec((None, k, tn), lambda j, i: (layer, 0, cb + j))],
        out_specs=pl.BlockSpec((tm, tn), lambda j, i: (i, j)),
        out_shape=jax.ShapeDtypeStruct((m, ncols), out_dtype),
        scratch_shapes=[pltpu.VMEM((k, tn), BF16)],
        compiler_params=_params(("parallel", "arbitrary")),
        name="proj",
    )(h2d, w_stack)


def _out_proj_kernel(ym_ref, ya_ref, w1_ref, w2_ref, x_ref, o_ref, w1b_ref, w2b_ref):
    @pl.when(pl.program_id(1) == 0)
    def _():
        w1b_ref[...] = w1_ref[...].astype(BF16)
        w2b_ref[...] = w2_ref[...].astype(BF16)

    acc = jnp.dot(ym_ref[...], w1b_ref[...], preferred_element_type=F32)
    acc = acc + jnp.dot(ya_ref[...], w2b_ref[...], preferred_element_type=F32)
    o_ref[...] = x_ref[...] + acc


def _out_proj(ym, ya, w_stack, layer, x2d, tm, tn):
    m, n = x2d.shape
    k1, k2 = ym.shape[1], ya.shape[1]
    assert k1 % k2 == 0
    return pl.pallas_call(
        _out_proj_kernel,
        grid=(n // tn, m // tm),
        in_specs=[pl.BlockSpec((tm, k1), lambda j, i: (i, 0)),
                  pl.BlockSpec((tm, k2), lambda j, i: (i, 0)),
                  pl.BlockSpec((None, k1, tn), lambda j, i: (layer, 0, j), pipeline_mode=pl.Buffered(1)),
                  pl.BlockSpec((None, k2, tn), lambda j, i: (layer, k1 // k2, j), pipeline_mode=pl.Buffered(1)),
                  pl.BlockSpec((tm, tn), lambda j, i: (i, j))],
        out_specs=pl.BlockSpec((tm, tn), lambda j, i: (i, j)),
        out_shape=jax.ShapeDtypeStruct((m, n), F32),
        scratch_shapes=[pltpu.VMEM((k1, tn), BF16), pltpu.VMEM((k2, tn), BF16)],
        compiler_params=_params(("parallel", "arbitrary")),
        name="out_proj",
    )(ym, ya, w_stack, w_stack, x2d)


def _xattn_kernel(q_ref, k_ref, v_ref, g_ref, o_ref):
    hd = XA_HEAD_DIM
    heads = [slice(h * hd, (h + 1) * hd) for h in range(XA_HEADS)]
    scores = [lax.dot_general(q_ref[0, :, hs], k_ref[0, :, hs], (((1,), (1,)), ((), ())),
                              preferred_element_type=F32) for hs in heads]
    probs, denoms = [], []
    for s in scores:
        s = s * (hd ** -0.5)
        p = jnp.exp(s - jnp.max(s, axis=-1, keepdims=True))
        denoms.append(jnp.sum(p, axis=-1, keepdims=True))
        probs.append(p.astype(BF16))
    outs = [jnp.dot(p, v_ref[0, :, hs], preferred_element_type=F32) for p, hs in zip(probs, heads)]
    for o, l, hs in zip(outs, denoms, heads):
        o_ref[0, :, hs] = (o / l * _silu(g_ref[0, :, hs].astype(F32))).astype(o_ref.dtype)


def _xattn(proj, kv, q_col, gate_col, ts):
    b, s, _ = proj.shape
    w = XA_WIDTH
    qb, gb = q_col // w, gate_col // w
    return pl.pallas_call(
        _xattn_kernel,
        grid=(b, s // ts),
        in_specs=[pl.BlockSpec((1, ts, w), lambda bi, t: (bi, t, qb)),
                  pl.BlockSpec((1, N_MEM, w), lambda bi, t: (bi, 0, 0)),
                  pl.BlockSpec((1, N_MEM, w), lambda bi, t: (bi, 0, 1)),
                  pl.BlockSpec((1, ts, w), lambda bi, t: (bi, t, gb))],
        out_specs=pl.BlockSpec((1, ts, w), lambda bi, t: (bi, t, 0)),
        out_shape=jax.ShapeDtypeStruct((b, s, XA_WIDTH), BF16),
        compiler_params=_params(("parallel", "parallel")),
        name="mem_xattn",
    )(proj, kv, kv, proj)


POOL_TILE = 128
POOL_SUBTILES = 4
POOL_HALO = 64


def _pool_kernel(u_ref, g_ref, wg_ref, sc_ref, o_ref, ext_ref):
    s = u_ref.shape[1]
    gw = u_ref.shape[2]
    grp = pl.program_id(1)
    half = jnp.left_shift(1, grp)
    ext_ref[0:POOL_HALO, :] = jnp.zeros((POOL_HALO, gw), BF16)
    ext_ref[POOL_HALO + s:POOL_HALO + s + POOL_HALO, :] = jnp.zeros((POOL_HALO, gw), BF16)
    ext_ref[POOL_HALO:POOL_HALO + s, :] = u_ref[0]

    kk = POOL_TILE + 2 * POOL_HALO
    r = lax.broadcasted_iota(jnp.int32, (POOL_TILE, kk), 0)
    c = lax.broadcasted_iota(jnp.int32, (POOL_TILE, kk), 1)
    d = c - POOL_HALO - r
    band = jnp.where((d >= -half) & (d < half), 1.0, 0.0).astype(BF16)
    wg = wg_ref[0]
    scale = sc_ref[0]

    rows = POOL_SUBTILES * POOL_TILE

    def body(i, carry):
        base = pl.multiple_of(i * rows, rows)
        pooled = []
        for sub in range(POOL_SUBTILES):
            t0 = base + sub * POOL_TILE
            wsum = jnp.dot(band, ext_ref[pl.ds(t0, kk), :], preferred_element_type=F32)
            tok = t0 + lax.broadcasted_iota(jnp.int32, (POOL_TILE, gw), 0)
            cnt = (jnp.minimum(tok + half, s) - jnp.maximum(tok - half, 0)).astype(F32)
            centre = ext_ref[pl.ds(t0 + POOL_HALO, POOL_TILE), :].astype(F32)
            pooled.append((wsum / cnt - centre).astype(BF16))
        group = 2
        mixed = [jnp.dot(jnp.concatenate(pooled[k:k + group], axis=0), wg, preferred_element_type=F32)
                 for k in range(0, POOL_SUBTILES, group)]
        for k, mx in enumerate(mixed):
            r0 = base + k * group * POOL_TILE
            gate = g_ref[0, pl.ds(r0, group * POOL_TILE), :].astype(F32)
            o_ref[0, pl.ds(r0, group * POOL_TILE), :] = (mx * scale * _silu(gate)).astype(o_ref.dtype)
        return carry

    lax.fori_loop(0, s // rows, body, 0)


def _pool_mix(proj, wg, scale, gate_col):
    b, s, _ = proj.shape
    gw = POOL_GROUP_WIDTH
    gb = gate_col // gw
    return pl.pallas_call(
        _pool_kernel,
        grid=(b, POOL_GROUPS),
        in_specs=[pl.BlockSpec((1, s, gw), lambda bi, g: (bi, 0, g)),
                  pl.BlockSpec((1, s, gw), lambda bi, g: (bi, 0, gb + g)),
                  pl.BlockSpec((1, gw, gw), lambda bi, g: (g, 0, 0)),
                  pl.BlockSpec((1, 1, gw), lambda bi, g: (g, 0, 0))],
        out_specs=pl.BlockSpec((1, s, gw), lambda bi, g: (bi, 0, g)),
        out_shape=jax.ShapeDtypeStruct((b, s, POOL_WIDTH), BF16),
        scratch_shapes=[pltpu.VMEM((s + 2 * POOL_HALO, gw), BF16)],
        compiler_params=_params(("parallel", "parallel")),
        name="pool_mix",
    )(proj, proj, wg, scale.reshape(POOL_GROUPS, 1, gw))


CONV_PAD = 8
CONV_HEADS_PER_STEP = 4


def _dn_prep_kernel(x_ref, cw_ref, *refs, normalize, scale, emit_t):
    if emit_t:
        o_ref, ot_ref, ext_ref = refs
    else:
        o_ref, ext_ref = refs
    s = x_ref.shape[1]
    width = x_ref.shape[2]
    hd = DN_HEAD_DIM
    for hl in range(width // hd):
        ext_ref[hl, 0:CONV_PAD, :] = jnp.zeros((CONV_PAD, hd), F32)
        ext_ref[hl, CONV_PAD + s:2 * CONV_PAD + s, :] = jnp.zeros((CONV_PAD, hd), F32)
        ext_ref[hl, CONV_PAD:CONV_PAD + s, :] = x_ref[0, :, hl * hd:(hl + 1) * hd].astype(F32)
    cw = cw_ref[...]
    tile = min(s, 256)

    def body(i, carry):
        t0 = pl.multiple_of(i * tile, tile)
        for hl in range(width // hd):
            lanes = slice(hl * hd, (hl + 1) * hd)
            acc = None
            for tap in range(DN_CONV_TAPS):
                off = CONV_PAD + tap - DN_CONV_TAPS // 2
                term = ext_ref[hl, pl.ds(t0 + off, tile), :] * cw[tap:tap + 1, lanes]
                acc = term if acc is None else acc + term
            y = _silu(acc)
            if normalize:
                y = y * lax.rsqrt(jnp.sum(y * y, axis=-1, keepdims=True) + EPS)
                if scale != 1.0:
                    y = y * scale
            o_ref[0, pl.ds(t0, tile), lanes] = y.astype(o_ref.dtype)
            if emit_t:
                for h in range(tile // DN_PAIR):
                    ot_ref[0, hl, i * (tile // DN_PAIR) + h] = (
                        y[h * DN_PAIR:(h + 1) * DN_PAIR, :].T.astype(ot_ref.dtype))
        return carry

    lax.fori_loop(0, s // tile, body, 0)


def _dn_prep(proj, conv_w, col0, ncols, normalize, scale, emit_t):
    b, s, _ = proj.shape
    hd = DN_HEAD_DIM
    width = CONV_HEADS_PER_STEP * hd
    nh = ncols // hd
    cb = col0 // width
    out_shape = [jax.ShapeDtypeStruct((b, s, ncols), BF16)]
    out_specs = [pl.BlockSpec((1, s, width), lambda bi, h: (bi, 0, h))]
    if emit_t:
        out_shape.append(jax.ShapeDtypeStruct((b, nh, s // DN_PAIR, hd, DN_PAIR), BF16))
        out_specs.append(pl.BlockSpec((1, CONV_HEADS_PER_STEP, s // DN_PAIR, hd, DN_PAIR),
                                      lambda bi, h: (bi, h, 0, 0, 0)))
    res = pl.pallas_call(
        functools.partial(_dn_prep_kernel, normalize=normalize, scale=scale, emit_t=emit_t),
        grid=(b, ncols // width),
        in_specs=[pl.BlockSpec((1, s, width), lambda bi, h: (bi, 0, cb + h)),
                  pl.BlockSpec((DN_CONV_TAPS, width), lambda bi, h: (0, cb + h))],
        out_specs=out_specs,
        out_shape=out_shape,
        scratch_shapes=[pltpu.VMEM((CONV_HEADS_PER_STEP, s + 2 * CONV_PAD, hd), F32)],
        compiler_params=_params(("parallel", "parallel")),
        name="dn_conv",
    )(proj, conv_w)
    return res if emit_t else res[0]


def _dn_gates_kernel(ba_ref, alog_ref, dtb_ref, g_ref, beta_ref, gt_ref, et_ref):
    s = ba_ref.shape[1]
    nh2 = 2 * DN_V_HEADS
    ba = ba_ref[0]
    beta_ref[0] = jax.nn.sigmoid(ba[:, :nh2])
    z = ba[:, nh2:] + dtb_ref[...]
    softplus = jnp.maximum(z, 0.0) + jnp.log(1.0 + jnp.exp(-jnp.abs(z)))
    g = -jnp.exp(alog_ref[...]) * softplus
    r = lax.broadcasted_iota(jnp.int32, (DN_CHUNK, DN_CHUNK), 0)
    c = lax.broadcasted_iota(jnp.int32, (DN_CHUNK, DN_CHUNK), 1)
    lower = jnp.where(r >= c, 1.0, 0.0).astype(F32)
    upper = jnp.where(r <= c, 1.0, 0.0).astype(F32)
    is_fwd = lax.broadcasted_iota(jnp.int32, (DN_CHUNK, nh2), 1) < DN_V_HEADS
    for n in range(s // DN_CHUNK):
        gc = g[n * DN_CHUNK:(n + 1) * DN_CHUNK, :]
        pre = jnp.dot(lower, gc, preferred_element_type=F32, precision=lax.Precision.HIGHEST)
        suf = jnp.dot(upper, gc, preferred_element_type=F32, precision=lax.Precision.HIGHEST)
        g_ref[0, n * DN_CHUNK:(n + 1) * DN_CHUNK, :] = jnp.where(is_fwd, pre, suf)
        g_tot = jnp.sum(gc, axis=0, keepdims=True)
        gt_ref[0, n:n + 1, :] = g_tot
        et_ref[0, n:n + 1, :] = jnp.exp(g_tot)


def _dn_gates(ba, a_log, dt_bias):
    b, s, w = ba.shape
    nh2 = 2 * DN_V_HEADS
    nchunk = s // DN_CHUNK
    return pl.pallas_call(
        _dn_gates_kernel,
        grid=(b,),
        in_specs=[pl.BlockSpec((1, s, w), lambda bi: (bi, 0, 0)),
                  pl.BlockSpec((1, nh2), lambda bi: (0, 0)),
                  pl.BlockSpec((1, nh2), lambda bi: (0, 0))],
        out_specs=[pl.BlockSpec((1, s, nh2), lambda bi: (bi, 0, 0)),
                   pl.BlockSpec((1, s, nh2), lambda bi: (bi, 0, 0)),
                   pl.BlockSpec((1, nchunk, nh2), lambda bi: (bi, 0, 0)),
                   pl.BlockSpec((1, nchunk, nh2), lambda bi: (bi, 0, 0))],
        out_shape=[jax.ShapeDtypeStruct((b, s, nh2), F32),
                   jax.ShapeDtypeStruct((b, s, nh2), F32),
                   jax.ShapeDtypeStruct((b, nchunk, nh2), F32),
                   jax.ShapeDtypeStruct((b, nchunk, nh2), F32)],
        compiler_params=_params(("parallel",)),
        name="dn_gates",
    )(ba, a_log.reshape(1, nh2), dt_bias.reshape(1, nh2))


N_UNITS = 4
DN_HEADS_PER_STEP = 2
ROW_G, ROW_BETA, ROW_GT = 0, 4, 8


def _dn_core_kernel(q_ref, k_ref, kt_ref, v_ref, r_ref, et_ref, gate_ref, nw_ref, o_ref,
                    u_sc, w_sc, a_sc, qd_sc, kd_sc, st_ref, of_ref, ob_ref):
    s = q_ref.shape[1]
    hd = DN_HEAD_DIM
    pr = DN_PAIR
    npair = s // pr
    nchunk = 2 * npair

    frow = lax.broadcasted_iota(jnp.int32, (DN_CHUNK, pr), 0)
    fcol = lax.broadcasted_iota(jnp.int32, (DN_CHUNK, pr), 1)
    left = fcol < DN_CHUNK
    ccol = jnp.where(left, fcol, fcol - DN_CHUNK)
    masks = {False: (frow >= ccol, frow > ccol),
             True: (frow <= ccol, frow < ccol)}

    def fold(m):
        return jnp.where(left, m[:DN_CHUNK], m[DN_CHUNK:])

    zeros_half = jnp.zeros((DN_CHUNK, hd), F32)
    zeros_blk = jnp.zeros((DN_CHUNK, pr), BF16)
    zeros_rhs = jnp.zeros((DN_CHUNK, 2 * hd), BF16)
    eye_folded = jnp.where((fcol == frow) | (fcol == frow + DN_CHUNK), 1.0, 0.0).astype(F32)

    def mm(a, b):
        return jnp.dot(a, b, preferred_element_type=F32)

    nheads = DN_HEADS_PER_STEP
    half = npair // 2
    nw = nw_ref[...]

    FWD, BWD = (0, 1), (2, 3)

    def solve_pairs(specs):
        heads = []
        for pi, subs in specs:
            t0 = pl.multiple_of(pi * pr, pr)
            for hl in range(nheads):
                heads.append((hl, pi, subs, q_ref[0, pl.ds(t0, pr), hl * hd:(hl + 1) * hd],
                              k_ref[0, pl.ds(t0, pr), hl * hd:(hl + 1) * hd], kt_ref[0, hl, pi],
                              v_ref[0, pl.ds(t0, pr), 2 * hl * hd:2 * (hl + 1) * hd], r_ref[0, hl, pi]))
        yield
        both = [mm(jnp.concatenate([q2, k2], axis=0), kt2) for _, _, _, q2, k2, kt2, _, _ in heads]
        yield
        chains = []
        for (hl, pi, subs, q2, k2, kt2, v2, rows), qkk in zip(heads, both):
            qk, kk = fold(qkk[:pr]), fold(qkk[pr:])
            q2f, k2f, kt2f = q2.astype(F32), k2.astype(F32), kt2.astype(F32)
            for sub in subs:
                unit = hl * N_UNITS + sub
                incl, strict = masks[sub >= 2]
                g_row = rows[ROW_G + sub:ROW_G + sub + 1]
                b_row = rows[ROW_BETA + sub:ROW_BETA + sub + 1]
                gt_row = rows[ROW_GT + sub:ROW_GT + sub + 1]
                m1 = jnp.broadcast_to(g_row, (pr, pr)).T
                diff = fold(m1) - g_row
                dm = jnp.where(incl, jnp.exp(jnp.where(incl, diff, 0.0)), 0.0)
                kt_mat = jnp.where(strict, kk * dm, 0.0) * b_row
                e1 = jnp.exp(m1)
                vh = sub % 2
                a_sc[unit, pi] = (qk * dm * b_row).astype(BF16)
                qd_sc[unit, pi] = (q2f * e1).astype(BF16)
                kd_sc[unit, pi] = (kt2f * (b_row * jnp.exp(gt_row - g_row))).astype(BF16)
                rhs = jnp.concatenate([v2[:, vh * hd:(vh + 1) * hd], (k2f * e1).astype(BF16)], axis=1)
                chains.append(dict(
                    unit=unit, pi=pi, p=-kt_mat, x=eye_folded,
                    rhs=jnp.concatenate(
                        [jnp.concatenate([rhs[DN_CHUNK:], zeros_rhs], axis=1),
                         jnp.concatenate([zeros_rhs, rhs[:DN_CHUNK]], axis=1)], axis=0)))
        for stage in range(6):
            outs = []
            for ch in chains:
                pb, xb = ch["p"].astype(BF16), ch["x"].astype(BF16)
                blk_a = jnp.where(left, pb, xb)
                blk_b = jnp.where(left, xb, pb)
                outs.append(mm(pb, jnp.concatenate(
                    [jnp.concatenate([blk_a, zeros_blk], axis=1),
                     jnp.concatenate([zeros_blk, blk_b], axis=1)], axis=0)))
            yield
            for ch, out in zip(chains, outs):
                ch["x"] = ch["x"] + jnp.where(left, out[:, pr:], out[:, :pr])
                if stage < 5:
                    ch["p"] = jnp.where(left, out[:, :pr], out[:, pr:])
        sols = [mm(ch["x"].astype(BF16), ch["rhs"]) for ch in chains]
        yield
        for ch, sol in zip(chains, sols):
            unit, pi = ch["unit"], ch["pi"]
            u_sc[unit, pi] = jnp.concatenate([sol[:, 2 * hd:3 * hd], sol[:, :hd]], axis=0)
            w_sc[unit, pi] = jnp.concatenate([sol[:, 3 * hd:], sol[:, hd:2 * hd]], axis=0).astype(BF16)

    def recur_step(it, finalize):
        units = []
        for unit in range(nheads * N_UNITS):
            hl, sub = divmod(unit, N_UNITS)
            backward = sub >= 2
            pi = (npair - 1 - it) if backward else it
            units.append(dict(
                unit=unit, backward=backward, pi=pi, vhead=2 * hl + sub % 2,
                u=u_sc[unit, pi], w=w_sc[unit, pi], a=a_sc[unit, pi],
                qd=qd_sc[unit, pi], kd=kd_sc[unit, pi], st=st_ref[unit], outs=[None, None],
                decay=[et_ref[0, hl, pl.ds(sub * nchunk + 2 * pi + c, 1), :] for c in range(2)]))
        if finalize:
            for un in units:
                t0 = pl.multiple_of(un["pi"] * pr, pr)
                lanes = slice(un["vhead"] * hd, (un["vhead"] + 1) * hd)
                if un["backward"]:
                    un["other"] = of_ref[un["vhead"], pl.ds(t0, pr), :]
                else:
                    un["other"] = ob_ref[un["vhead"], pl.ds(pl.multiple_of((un["pi"] - half) * pr, pr), pr), :]
                un["gate"] = gate_ref[0, pl.ds(t0, pr), lanes]
        yield
        for step in range(2):
            for un in units:
                c = (1 - step) if un["backward"] else step
                un["c"] = c
                un["rows"] = slice(c * DN_CHUNK, (c + 1) * DN_CHUNK)
            rss = [mm(jnp.concatenate([un["w"][un["rows"]], un["qd"][un["rows"]]], axis=0),
                      un["st"].astype(BF16)) for un in units]
            yield
            for un, rs in zip(units, rss):
                halves = [zeros_half, zeros_half]
                halves[un["c"]] = un["u"][un["rows"]] - rs[:DN_CHUNK]
                un["v_full"] = jnp.concatenate(halves, axis=0).astype(BF16)
                un["qs"] = rs[DN_CHUNK:]
            upd = [mm(un["kd"], un["v_full"]) for un in units]
            intra = [mm(un["a"], un["v_full"]) for un in units]
            yield
            for un, o_in, st_up in zip(units, intra, upd):
                un["outs"][un["c"]] = un["qs"] + o_in
                un["st"] = un["st"] * un["decay"][un["c"]] + st_up
        for un in units:
            t0 = pl.multiple_of(un["pi"] * pr, pr)
            o_pair = jnp.concatenate(un["outs"], axis=0)
            if finalize:
                o = o_pair + un["other"]
                o = o * lax.rsqrt(jnp.mean(o * o, axis=-1, keepdims=True) + EPS) * nw
                lanes = slice(un["vhead"] * hd, (un["vhead"] + 1) * hd)
                o_ref[0, pl.ds(t0, pr), lanes] = (o * _silu(un["gate"].astype(F32))).astype(o_ref.dtype)
            elif un["backward"]:
                ob_ref[un["vhead"], pl.ds(pl.multiple_of((un["pi"] - half) * pr, pr), pr), :] = o_pair
            else:
                of_ref[un["vhead"], pl.ds(t0, pr), :] = o_pair
            st_ref[un["unit"]] = un["st"]

    def drain(gen):
        for _ in gen:
            pass

    def ahead(t):
        return [(2 * t, FWD), (2 * t + 1, FWD), (npair - 1 - 2 * t, BWD), (npair - 2 - 2 * t, BWD)]

    assert half % 2 == 0
    st_ref[...] = jnp.zeros(st_ref.shape, F32)
    drain(solve_pairs(ahead(0)))

    def merged(t, carry, finalize):
        solve = solve_pairs(ahead(t + 1))
        first, second = recur_step(2 * t, finalize), recur_step(2 * t + 1, finalize)
        next(solve)
        for _ in range(4):
            next(first)
            next(solve)
        next(first)
        drain(first)
        for _ in range(4):
            next(second)
            next(solve)
        next(second)
        drain(solve)
        drain(second)
        return carry

    lax.fori_loop(0, half // 2, functools.partial(merged, finalize=False), 0)
    lax.fori_loop(half // 2, npair // 2 - 1, functools.partial(merged, finalize=True), 0)
    drain(recur_step(npair - 2, True))
    drain(recur_step(npair - 1, True))


def _dn_core(qn, kn, knt, vs, rows, etot, proj, norm_w, gate_col):
    b, s, _ = qn.shape
    hd = DN_HEAD_DIM
    npair = s // DN_PAIR
    nh = DN_HEADS_PER_STEP
    nu = nh * N_UNITS
    gb = gate_col // (2 * nh * hd)
    return pl.pallas_call(
        _dn_core_kernel,
        grid=(b, DN_QK_HEADS // nh),
        in_specs=[pl.BlockSpec((1, s, nh * hd), lambda bi, j: (bi, 0, j)),
                  pl.BlockSpec((1, s, nh * hd), lambda bi, j: (bi, 0, j)),
                  pl.BlockSpec((1, nh, npair, hd, DN_PAIR), lambda bi, j: (bi, j, 0, 0, 0)),
                  pl.BlockSpec((1, s, 2 * nh * hd), lambda bi, j: (bi, 0, j)),
                  pl.BlockSpec((1, nh, npair, 16, DN_PAIR), lambda bi, j: (bi, j, 0, 0, 0)),
                  pl.BlockSpec((1, nh, N_UNITS * 2 * npair, hd), lambda bi, j: (bi, j, 0, 0)),
                  pl.BlockSpec((1, s, 2 * nh * hd), lambda bi, j: (bi, 0, gb + j)),
                  pl.BlockSpec((1, hd), lambda bi, j: (0, 0))],
        out_specs=pl.BlockSpec((1, s, 2 * nh * hd), lambda bi, j: (bi, 0, j)),
        out_shape=jax.ShapeDtypeStruct((b, s, DN_V_WIDTH), BF16),
        scratch_shapes=[pltpu.VMEM((nu, npair, DN_PAIR, hd), F32),
                        pltpu.VMEM((nu, npair, DN_PAIR, hd), BF16),
                        pltpu.VMEM((nu, npair, DN_CHUNK, DN_PAIR), BF16),
                        pltpu.VMEM((nu, npair, DN_PAIR, hd), BF16),
                        pltpu.VMEM((nu, npair, hd, DN_PAIR), BF16),
                        pltpu.VMEM((nu, hd, hd), F32),
                        pltpu.VMEM((2 * nh, s // 2, hd), F32),
                        pltpu.VMEM((2 * nh, s // 2, hd), F32)],
        compiler_params=_params(("parallel", "parallel")),
        name="dn_core",
    )(qn, kn, knt, vs, rows, etot, proj, norm_w.reshape(1, hd))


def _per_unit(a):
    b, t, _ = a.shape
    a = a.reshape(b, t, 2, DN_QK_HEADS, 2)
    return a.transpose(0, 3, 2, 4, 1).reshape(b, DN_QK_HEADS, N_UNITS, t)


def _deltanet_mixer(proj, ba, conv_w, a_log, dt_bias, norm_w, gate_col):
    b, s, _ = proj.shape
    npair = s // DN_PAIR
    nchunk = s // DN_CHUNK
    qn = _dn_prep(proj, conv_w, 0, DN_QK_WIDTH, True, DN_HEAD_DIM ** -0.5, False)
    kn, knt = _dn_prep(proj, conv_w, DN_QK_WIDTH, DN_QK_WIDTH, True, 1.0, True)
    vs = _dn_prep(proj, conv_w, 2 * DN_QK_WIDTH, DN_V_WIDTH, False, 1.0, False)
    g_cum, beta, g_tot, e_tot = _dn_gates(ba, a_log, dt_bias)
    gt_tok = jnp.broadcast_to(g_tot[:, :, None, :], (b, nchunk, DN_CHUNK, g_tot.shape[-1])).reshape(b, s, -1)
    rows = jnp.concatenate([_per_unit(g_cum), _per_unit(beta), _per_unit(gt_tok),
                            jnp.zeros((b, DN_QK_HEADS, N_UNITS, s), F32)], axis=2)
    rows = rows.reshape(b, DN_QK_HEADS, 16, npair, DN_PAIR).transpose(0, 1, 3, 2, 4)
    etot = _per_unit(e_tot).reshape(b, DN_QK_HEADS, N_UNITS * nchunk)
    etot = jnp.broadcast_to(etot[..., None], (b, DN_QK_HEADS, N_UNITS * nchunk, DN_HEAD_DIM))
    return _dn_core(qn, kn, knt, vs, rows, etot, proj, norm_w, gate_col)


def kernel(x, mem, norm_w, mem_norm_w, w_kv_mem, w_out, pool_w_in, pool_w_group, pool_scale,
           dn_w_in, dn_conv_w, dn_a_log, dn_dt_bias, dn_norm_w, final_norm_w):
    b, s, d = x.shape
    m = mem.shape[1]
    depth = norm_w.shape[0]
    x2d = x.reshape(b * s, d)
    mem2d = mem.reshape(b * m, d)
    tm = min(2048, b * s)
    tr = min(512, b * s)
    for i in range(depth):
        j = i // 2
        hm = _rmsnorm(mem2d, mem_norm_w[i], BF16, min(512, b * m))
        kv = _proj(hm, w_kv_mem, i, 0, 2 * XA_WIDTH, BF16, min(1024, b * m), 1024)
        kv = kv.reshape(b, m, 2 * XA_WIDTH)
        h = _rmsnorm(x2d, norm_w[i], BF16, tr)
        if i % 2 == 0:
            proj = _proj(h, pool_w_in, j, 0, pool_w_in.shape[2], BF16, tm, 1024)
            proj = proj.reshape(b, s, -1)
            gate_col = POOL_WIDTH + XA_WIDTH
            ym = _pool_mix(proj, pool_w_group[j].astype(BF16), pool_scale[j], gate_col)
            ya = _xattn(proj, kv, POOL_WIDTH, gate_col + POOL_WIDTH, min(512, s))
        else:
            proj = _proj(h, dn_w_in, j, 0, DN_MAIN, BF16, tm, 1024)
            proj = proj.reshape(b, s, -1)
            ba = _proj(h, dn_w_in, j, DN_MAIN, 4 * DN_V_HEADS, F32, tm, 4 * DN_V_HEADS)
            ba = ba.reshape(b, s, -1)
            gate_col = DN_CONV_CH + XA_WIDTH
            ym = _deltanet_mixer(proj, ba, dn_conv_w[j], dn_a_log[j], dn_dt_bias[j], dn_norm_w[j], gate_col)
            ya = _xattn(proj, kv, DN_CONV_CH, gate_col + DN_V_WIDTH, min(512, s))
        x2d = _out_proj(ym.reshape(b * s, -1), ya.reshape(b * s, -1), w_out, i, x2d, min(1024, b * s), 512)
    return _rmsnorm(x2d, final_norm_w, F32, tr).reshape(b, s, d)
```
